```python
import jax, jax.numpy as jnp
from jax import lax
import numpy as np

D_MODEL = 1024
BATCH = 4
SEQ = 8192
DEPTH = 1

D_MIX = D_MODEL
D_CONV = D_MIX // 2
D_ATTN = D_MIX - D_CONV
ATTN_HEAD_DIM = 64
N_ATTN_HEADS = D_ATTN // ATTN_HEAD_DIM
CONV_WIDTH = 31
D_FF = -(-8 * D_MODEL // (3 * 256)) * 256
Q_BLOCK = 128
EPS = 1e-6
D_IN = 2 * D_CONV + 3 * D_ATTN + N_ATTN_HEADS

kernel_name = "hymba_conformer_fox_sandwich"


def rmsnorm(x, g):
    xf = x.astype(jnp.float32)
    y = xf * lax.rsqrt(jnp.mean(xf * xf, axis=-1, keepdims=True) + EPS)
    return (y * g.astype(jnp.float32)).astype(x.dtype)


def layernorm(x, g, b):
    xf = x.astype(jnp.float32)
    mu = jnp.mean(xf, axis=-1, keepdims=True)
    var = jnp.mean(jnp.square(xf - mu), axis=-1, keepdims=True)
    y = (xf - mu) * lax.rsqrt(var + EPS)
    return (y * g.astype(jnp.float32) + b.astype(jnp.float32)).astype(x.dtype)


def causal_depthwise_conv(u, w, b):
    c = u.shape[-1]
    y = lax.conv_general_dilated(
        u, w[:, None, :].astype(u.dtype), window_strides=(1,),
        padding=[(CONV_WIDTH - 1, 0)],
        dimension_numbers=("NWC", "WIO", "NWC"),
        feature_group_count=c)
    return y + b.astype(u.dtype)


def forgetting_attention(q, k, v, log_f):
    b_, h_, s_, dh = q.shape
    c = jnp.cumsum(log_f, axis=-1)
    kpos = jnp.arange(s_)
    scale = dh ** -0.5

    def one_block(i):
        start = i * Q_BLOCK
        qb = lax.dynamic_slice_in_dim(q, start, Q_BLOCK, axis=2)
        cb = lax.dynamic_slice_in_dim(c, start, Q_BLOCK, axis=2)
        logits = jnp.einsum("bhqd,bhkd->bhqk", qb, k,
                            preferred_element_type=jnp.float32) * scale
        logits = logits + cb[..., :, None] - c[..., None, :]
        qpos = start + jnp.arange(Q_BLOCK)
        causal = kpos[None, :] <= qpos[:, None]
        logits = jnp.where(causal, logits, -jnp.inf)
        p = jax.nn.softmax(logits, axis=-1).astype(v.dtype)
        return jnp.einsum("bhqk,bhkd->bhqd", p, v)

    out = lax.map(one_block, jnp.arange(s_ // Q_BLOCK))
    return jnp.moveaxis(out, 0, 2).reshape(b_, h_, s_, dh)


def setup_inputs(seed: int = 0) -> dict:
    key = jax.random.key(seed)
    ks = jax.random.split(key, 16)
    f32 = jnp.float32

    def nrm(k, shape, scale):
        return jax.random.normal(k, shape, f32) * scale

    def gain(k, n):
        return 1.0 + nrm(k, (DEPTH, n), 0.02)

    return {
        "x": jax.random.normal(ks[0], (BATCH, SEQ, D_MODEL), f32),
        "g_mix_pre": gain(ks[1], D_MODEL),
        "w_in": nrm(ks[2], (DEPTH, D_MODEL, D_IN), D_MODEL ** -0.5),
        "b_forget": 3.0 + nrm(ks[3], (DEPTH, N_ATTN_HEADS), 0.5),
        "conv_w": nrm(ks[4], (DEPTH, CONV_WIDTH, D_CONV), CONV_WIDTH ** -0.5),
        "conv_b": nrm(ks[5], (DEPTH, D_CONV), 0.02),
        "conv_ln_g": gain(ks[6], D_CONV),
        "conv_ln_b": nrm(ks[7], (DEPTH, D_CONV), 0.02),
        "w_out": nrm(ks[8], (DEPTH, D_MIX, D_MODEL), D_MIX ** -0.5),
        "g_mix_post": gain(ks[9], D_MODEL),
        "g_ffn_pre": gain(ks[10], D_MODEL),
        "w_gate": nrm(ks[11], (DEPTH, D_MODEL, D_FF), D_MODEL ** -0.5),
        "w_up": nrm(ks[12], (DEPTH, D_MODEL, D_FF), D_MODEL ** -0.5),
        "w_down": nrm(ks[13], (DEPTH, D_FF, D_MODEL), D_FF ** -0.5),
        "g_ffn_post": gain(ks[14], D_MODEL),
    }


def reference(x, g_mix_pre, w_in, b_forget, conv_w, conv_b, conv_ln_g, conv_ln_b,
              w_out, g_mix_post, g_ffn_pre, w_gate, w_up, w_down, g_ffn_post):
    b_, s_, _ = x.shape
    splits = [D_CONV, 2 * D_CONV, 2 * D_CONV + D_ATTN,
              2 * D_CONV + 2 * D_ATTN, 2 * D_CONV + 3 * D_ATTN]

    def heads(t):
        return t.reshape(b_, s_, N_ATTN_HEADS, ATTN_HEAD_DIM).transpose(0, 2, 1, 3)

    for l in range(DEPTH):
        h = rmsnorm(x, g_mix_pre[l])
        proj = h @ w_in[l]
        a_val, a_gate, q, k, v, f_logit = jnp.split(proj, splits, axis=-1)

        u = a_val * jax.nn.sigmoid(a_gate)
        u = causal_depthwise_conv(u, conv_w[l], conv_b[l])
        conv_out = jax.nn.silu(layernorm(u, conv_ln_g[l], conv_ln_b[l]))

        log_f = jax.nn.log_sigmoid(
            f_logit.astype(jnp.float32) + b_forget[l].astype(jnp.float32))
        attn = forgetting_attention(heads(q), heads(k), heads(v),
                                    log_f.transpose(0, 2, 1))
        attn_out = attn.transpose(0, 2, 1, 3).reshape(b_, s_, D_ATTN)

        mix = jnp.concatenate([conv_out, attn_out], axis=-1) @ w_out[l]
        x = x + rmsnorm(mix, g_mix_post[l])

        h = rmsnorm(x, g_ffn_pre[l])
        y = (jax.nn.silu(h @ w_gate[l]) * (h @ w_up[l])) @ w_down[l]
        x = x + rmsnorm(y, g_ffn_post[l])
    return x
```

```python
import functools

import jax
import jax.numpy as jnp
from jax import lax
from jax.experimental import pallas as pl
from jax.experimental.pallas import tpu as pltpu

HEAD_DIM = 64
CONV_WIDTH = 31
EPS = 1e-6
LANES = 128
SUBLANES = 8
HALO_ROWS = 32
DECAY_SPLITS = 3
VMEM_LIMIT_BYTES = 56 * 1024 * 1024
NEG_BIG = -1e30

_f32 = jnp.float32
_bf16 = jnp.bfloat16


def _cparams(sem):
    return pltpu.CompilerParams(dimension_semantics=sem, vmem_limit_bytes=VMEM_LIMIT_BYTES)


def _resident(shape):
    nd = len(shape)
    return pl.BlockSpec(shape, lambda *_: (0,) * nd, pipeline_mode=pl.Buffered(1))


def _rms_scale(x):
    return lax.rsqrt(jnp.mean(x * x, axis=-1, keepdims=True) + EPS)


def _split3(x):
    hi = x.astype(_bf16)
    r1 = x - hi.astype(_f32)
    mid = r1.astype(_bf16)
    lo = (r1 - mid.astype(_f32)).astype(_bf16)
    return hi, mid, lo


def _in_proj_kernel(x_ref, g_ref, wn_ref, wt_ref, bf_ref, u_ref, k_ref, lf_ref, qt_ref, vt_ref,
                    *, d_conv, d_attn):
    x = x_ref[...]
    h = (x * _rms_scale(x) * g_ref[...]).astype(_bf16)
    nat = jnp.dot(h, wn_ref[...], preferred_element_type=_f32)
    a_val = nat[:, :d_conv]
    a_gate = nat[:, d_conv:2 * d_conv]
    u_ref[...] = (a_val * jax.nn.sigmoid(a_gate)).astype(u_ref.dtype)
    k_ref[...] = nat[:, 2 * d_conv:2 * d_conv + d_attn].astype(k_ref.dtype)
    z = nat[:, 2 * d_conv + d_attn:] + bf_ref[...]
    lf_ref[...] = jnp.minimum(z, 0.0) - jnp.log1p(jnp.exp(-jnp.abs(z)))
    tr = lax.dot_general(wt_ref[...], h, (((1,), (1,)), ((), ())), preferred_element_type=_f32)
    qt_ref[...] = tr[:d_attn].astype(qt_ref.dtype)
    vt_ref[...] = tr[d_attn:].astype(vt_ref.dtype)


def _in_proj(x, g, w_nat, w_tr, bf_ext, *, d_conv, d_attn, tm):
    b, s, d = x.shape
    n_nat = w_nat.shape[1]
    kern = functools.partial(_in_proj_kernel, d_conv=d_conv, d_attn=d_attn)
    return pl.pallas_call(
        kern,
        grid=(b, s // tm),
        in_specs=[
            pl.BlockSpec((None, tm, d), lambda i, j: (i, j, 0)),
            _resident((1, d)),
            _resident((d, n_nat)),
            _resident((2 * d_attn, d)),
            _resident((1, LANES)),
        ],
        out_specs=[
            pl.BlockSpec((None, tm, d_conv), lambda i, j: (i, j, 0)),
            pl.BlockSpec((None, tm, d_attn), lambda i, j: (i, j, 0)),
            pl.BlockSpec((None, tm, LANES), lambda i, j: (i, j, 0)),
            pl.BlockSpec((None, d_attn, tm), lambda i, j: (i, 0, j)),
            pl.BlockSpec((None, d_attn, tm), lambda i, j: (i, 0, j)),
        ],
        out_shape=[
            jax.ShapeDtypeStruct((b, s, d_conv), _bf16),
            jax.ShapeDtypeStruct((b, s, d_attn), _bf16),
            jax.ShapeDtypeStruct((b, s, LANES), _f32),
            jax.ShapeDtypeStruct((b, d_attn, s), _bf16),
            jax.ShapeDtypeStruct((b, d_attn, s), _bf16),
        ],
        compiler_params=_cparams(("parallel", "parallel")),
        name="in_proj",
    )(x, g, w_nat, w_tr, bf_ext)


def _decay_kernel(lf_ref, aug_ref, *, n_groups, n_cols):
    row = lax.broadcasted_iota(jnp.int32, (LANES, LANES), 0)
    col = lax.broadcasted_iota(jnp.int32, (LANES, LANES), 1)
    tri = (col <= row).astype(_bf16)
    piece = col % DECAY_SPLITS

    def body(gi, carry):
        r0 = pl.multiple_of(gi * LANES, LANES)
        hi, mid, lo = _split3(lf_ref[pl.ds(r0, LANES), :])
        local = (jnp.dot(tri, hi, preferred_element_type=_f32)
                 + jnp.dot(tri, mid, preferred_element_type=_f32)
                 + jnp.dot(tri, lo, preferred_element_type=_f32))
        c = local + carry
        nhi, nmid, nlo = _split3(-c)
        aug = jnp.where(piece == 0, nhi, jnp.where(piece == 1, nmid, nlo))
        aug_ref[pl.ds(r0, LANES), :] = jnp.where(col < n_cols, aug, jnp.zeros_like(aug))
        return c[LANES - 1:LANES, :]

    lax.fori_loop(0, n_groups, body, jnp.zeros((1, LANES), _f32))


def _decay_columns(lf, *, n_cols):
    b, s, _ = lf.shape
    kern = functools.partial(_decay_kernel, n_groups=s // LANES, n_cols=n_cols)
    return pl.pallas_call(
        kern,
        grid=(b,),
        in_specs=[pl.BlockSpec((None, s, LANES), lambda i: (i, 0, 0))],
        out_specs=pl.BlockSpec((None, s, LANES), lambda i: (i, 0, 0)),
        out_shape=jax.ShapeDtypeStruct((b, s, LANES), _bf16),
        compiler_params=_cparams(("parallel",)),
        name="decay_columns",
    )(lf)


def _conv_kernel(cur_ref, halo_ref, w_ref, cb_ref, lg_ref, lb_ref, o_ref, win_ref, *, tc, chunk):
    first = pl.program_id(1) == 0
    halo = halo_ref[...].astype(_f32)
    win_ref[0, 0:HALO_ROWS, :] = jnp.where(first, jnp.zeros_like(halo), halo)
    win_ref[0, HALO_ROWS:, :] = cur_ref[...].astype(_f32)
    n_shift = tc + HALO_ROWS - SUBLANES
    for r in range(1, SUBLANES):
        win_ref[r, 0:n_shift, :] = win_ref[0, r:r + n_shift, :]
    lead = HALO_ROWS - (CONV_WIDTH - 1)

    def body(ci, _):
        r0 = pl.multiple_of(ci * chunk, chunk)
        acc = jnp.broadcast_to(cb_ref[...], (chunk, cb_ref.shape[1]))
        for k in range(CONV_WIDTH):
            phase, base = (lead + k) % SUBLANES, (lead + k) // SUBLANES * SUBLANES
            tap = win_ref[phase, pl.ds(pl.multiple_of(r0 + base, SUBLANES), chunk), :]
            acc = acc + w_ref[k:k + 1, :] * tap
        mu = jnp.mean(acc, axis=-1, keepdims=True)
        cen = acc - mu
        var = jnp.mean(cen * cen, axis=-1, keepdims=True)
        y = cen * lax.rsqrt(var + EPS) * lg_ref[...] + lb_ref[...]
        o_ref[pl.ds(r0, chunk), :] = (y * jax.nn.sigmoid(y)).astype(o_ref.dtype)
        return 0

    lax.fori_loop(0, tc // chunk, body, 0)


def _conv_module(u, conv_w, conv_b, ln_g, ln_b, *, tc, chunk):
    b, s, c = u.shape
    halo_blocks = tc // HALO_ROWS
    kern = functools.partial(_conv_kernel, tc=tc, chunk=chunk)
    return pl.pallas_call(
        kern,
        grid=(b, s // tc),
        in_specs=[
            pl.BlockSpec((None, tc, c), lambda i, j: (i, j, 0)),
            pl.BlockSpec((None, HALO_ROWS, c), lambda i, j: (i, jnp.maximum(j * halo_blocks - 1, 0), 0)),
            _resident((CONV_WIDTH, c)),
            _resident((1, c)),
            _resident((1, c)),
            _resident((1, c)),
        ],
        out_specs=pl.BlockSpec((None, tc, c), lambda i, j: (i, j, 0)),
        out_shape=jax.ShapeDtypeStruct((b, s, c), _bf16),
        scratch_shapes=[pltpu.VMEM((SUBLANES, tc + HALO_ROWS, c), _f32)],
        compiler_params=_cparams(("parallel", "parallel")),
        name="conv_module",
    )(u, u, conv_w, conv_b, ln_g, ln_b)


def _attn_kernel(qt_ref, k_ref, aug_ref, vt_ref, o_ref, acc_ref, m_ref, *, tq, tk):
    pair = pl.program_id(1)
    qi = pl.program_id(2)
    t0 = qi * tq
    hd = HEAD_DIM
    acc_rows = acc_ref.shape[1]

    row = lax.broadcasted_iota(jnp.int32, (LANES, tq), 0)
    qt = qt_ref[...]
    ones_rows = jnp.ones((acc_rows - hd, tk), _bf16)
    q_ext = []
    for hh in range(2):
        head = 2 * pair + hh
        top = jnp.where((row >= hh * hd) & (row < (hh + 1) * hd), qt, jnp.zeros_like(qt))
        sel = ((row >= DECAY_SPLITS * head) & (row < DECAY_SPLITS * (head + 1))).astype(_bf16)
        q_ext.append(jnp.concatenate([top, sel], axis=0))

    acc_ref[...] = jnp.zeros_like(acc_ref)
    m_ref[...] = jnp.full_like(m_ref, NEG_BIG)

    def step(s0, masked):
        kk = jnp.concatenate([k_ref[pl.ds(s0, tk), :], aug_ref[pl.ds(s0, tk), :]], axis=1)
        vt = vt_ref[:, pl.ds(s0, tk)]
        if masked:
            key_pos = s0 + lax.broadcasted_iota(jnp.int32, (tk, tq), 0)
            qry_pos = t0 + lax.broadcasted_iota(jnp.int32, (tk, tq), 1)
            valid = key_pos <= qry_pos
        for hh in range(2):
            st = jnp.dot(kk, q_ext[hh], preferred_element_type=_f32)
            if masked:
                st = jnp.where(valid, st, NEG_BIG)
            m_old = m_ref[hh]
            m_new = jnp.maximum(m_old, jnp.max(st, axis=0, keepdims=True))
            p = jnp.exp(st - m_new).astype(_bf16)
            alpha = jnp.exp(m_old - m_new)
            v_ext = jnp.concatenate([vt[hh * hd:(hh + 1) * hd, :], ones_rows], axis=0)
            acc_ref[hh] = alpha * acc_ref[hh] + jnp.dot(v_ext, p, preferred_element_type=_f32)
            m_ref[hh] = m_new

    def full_body(j, _):
        step(pl.multiple_of(j * tk, tk), masked=False)
        return 0

    lax.fori_loop(0, t0 // tk, full_body, 0)
    for d in range(tq // tk):
        step(pl.multiple_of(t0 + d * tk, tk), masked=True)

    for hh in range(2):
        acc = acc_ref[hh]
        inv_l = 1.0 / acc[hd:hd + 1, :]
        o_ref[hh * hd:(hh + 1) * hd, :] = (acc[:hd, :] * inv_l).astype(o_ref.dtype)


def _attention(qt, k, aug, vt, *, tq, tk):
    b, da, s = qt.shape
    n_pairs = da // LANES
    acc_rows = HEAD_DIM + 16
    kern = functools.partial(_attn_kernel, tq=tq, tk=tk)
    return pl.pallas_call(
        kern,
        grid=(b, n_pairs, s // tq),
        in_specs=[
            pl.BlockSpec((None, LANES, tq), lambda i, p, j: (i, p, j)),
            pl.BlockSpec((None, s, LANES), lambda i, p, j: (i, 0, p)),
            pl.BlockSpec((None, s, LANES), lambda i, p, j: (i, 0, 0)),
            pl.BlockSpec((None, LANES, s), lambda i, p, j: (i, p, 0)),
        ],
        out_specs=pl.BlockSpec((None, LANES, tq), lambda i, p, j: (i, p, j)),
        out_shape=jax.ShapeDtypeStruct((b, da, s), _bf16),
        scratch_shapes=[pltpu.VMEM((2, acc_rows, tq), _f32), pltpu.VMEM((2, 1, tq), _f32)],
        compiler_params=_cparams(("parallel", "parallel", "arbitrary")),
        name="attention",
    )(qt, k, aug, vt)


def _mix_ffn_kernel(x_ref, conv_ref, at_ref, woc_ref, woa_ref, gpost_ref, gpre_ref, wg_ref, wu_ref,
                    wd_ref, gout_ref, o_ref):
    mix = jnp.dot(conv_ref[...], woc_ref[...], preferred_element_type=_f32)
    mix = mix + lax.dot_general(at_ref[...], woa_ref[...], (((0,), (0,)), ((), ())),
                                preferred_element_type=_f32)
    x1 = x_ref[...] + mix * _rms_scale(mix) * gpost_ref[...]
    h = (x1 * _rms_scale(x1) * gpre_ref[...]).astype(_bf16)
    gate = jnp.dot(h, wg_ref[...], preferred_element_type=_f32)
    up = jnp.dot(h, wu_ref[...], preferred_element_type=_f32)
    act = (gate * jax.nn.sigmoid(gate) * up).astype(_bf16)
    y = jnp.dot(act, wd_ref[...], preferred_element_type=_f32)
    o_ref[...] = x1 + y * _rms_scale(y) * gout_ref[...]


def _mix_ffn(x, conv_out, attn_t, wo_c, wo_a, g_post, g_pre, w_gate, w_up, w_down, g_out, *, tm):
    b, s, d = x.shape
    dc = conv_out.shape[2]
    da = attn_t.shape[1]
    dff = w_gate.shape[1]
    return pl.pallas_call(
        _mix_ffn_kernel,
        grid=(b, s // tm),
        in_specs=[
            pl.BlockSpec((None, tm, d), lambda i, j: (i, j, 0)),
            pl.BlockSpec((None, tm, dc), lambda i, j: (i, j, 0)),
            pl.BlockSpec((None, da, tm), lambda i, j: (i, 0, j)),
            _resident((dc, d)),
            _resident((da, d)),
            _resident((1, d)),
            _resident((1, d)),
            _resident((d, dff)),
            _resident((d, dff)),
            _resident((dff, d)),
            _resident((1, d)),
        ],
        out_specs=pl.BlockSpec((None, tm, d), lambda i, j: (i, j, 0)),
        out_shape=jax.ShapeDtypeStruct((b, s, d), x.dtype),
        compiler_params=_cparams(("parallel", "parallel")),
        name="mix_ffn",
    )(x, conv_out, attn_t, wo_c, wo_a, g_post, g_pre, w_gate, w_up, w_down, g_out)


def _layer(x, g_mix_pre, w_in, b_forget, conv_w, conv_b, conv_ln_g, conv_ln_b, w_out, g_mix_post,
           g_ffn_pre, w_gate, w_up, w_down, g_ffn_post):
    b, s, d = x.shape
    d_conv = conv_w.shape[1]
    n_heads = b_forget.shape[0]
    d_attn = n_heads * HEAD_DIM
    n_cols = DECAY_SPLITS * n_heads
    assert 2 * d_conv + 3 * d_attn + n_heads == w_in.shape[1]
    assert n_cols <= LANES and d_attn % LANES == 0

    tm = min(512, s)
    tq = min(256, s)
    tk = tq
    assert s % tm == 0 and s % tq == 0 and s % LANES == 0

    o_q, o_k, o_v, o_f = 2 * d_conv, 2 * d_conv + d_attn, 2 * d_conv + 2 * d_attn, 2 * d_conv + 3 * d_attn
    w_f = jnp.repeat(w_in[:, o_f:], DECAY_SPLITS, axis=1)
    w_f = jnp.pad(w_f, ((0, 0), (0, LANES - n_cols)))
    w_nat = jnp.concatenate([w_in[:, :o_q], w_in[:, o_k:o_v], w_f], axis=1).astype(_bf16)
    w_tr = jnp.concatenate([w_in[:, o_q:o_k] * (HEAD_DIM ** -0.5), w_in[:, o_v:o_f]], axis=1).T.astype(_bf16)
    bf_ext = jnp.pad(jnp.repeat(b_forget.astype(_f32), DECAY_SPLITS), (0, LANES - n_cols))[None, :]

    row = lambda v: v.astype(_f32)[None, :]
    u, k, lf, qt, vt = _in_proj(x, row(g_mix_pre), w_nat, w_tr, bf_ext, d_conv=d_conv, d_attn=d_attn, tm=tm)
    aug = _decay_columns(lf, n_cols=n_cols)
    conv_out = _conv_module(u, conv_w.astype(_f32), row(conv_b), row(conv_ln_g), row(conv_ln_b),
                            tc=tm, chunk=32)
    attn_t = _attention(qt, k, aug, vt, tq=tq, tk=tk)
    return _mix_ffn(x, conv_out, attn_t, w_out[:d_conv].astype(_bf16), w_out[d_conv:].astype(_bf16),
                    row(g_mix_post), row(g_ffn_pre), w_gate.astype(_bf16), w_up.astype(_bf16),
                    w_down.astype(_bf16), row(g_ffn_post), tm=tm)


def kernel(x, g_mix_pre, w_in, b_forget, conv_w, conv_b, conv_ln_g, conv_ln_b, w_out, g_mix_post,
           g_ffn_pre, w_gate, w_up, w_down, g_ffn_post):
    for l in range(g_mix_pre.shape[0]):
        x = _layer(x, g_mix_pre[l], w_in[l], b_forget[l], conv_w[l], conv_b[l], conv_ln_g[l],
                   conv_ln_b[l], w_out[l], g_mix_post[l], g_ffn_pre[l], w_gate[l], w_up[l],
                   w_down[l], g_ffn_post[l])
    return x
```

```python
import functools

import jax
import jax.numpy as jnp
from jax import lax
from jax.experimental import pallas as pl
from jax.experimental.pallas import tpu as pltpu

HEAD_DIM = 64
CONV_WIDTH = 31
EPS = 1e-6
LANES = 128
SUBLANES = 8
BF16_ROWS = 16
HALO_ROWS = 32
DECAY_SPLITS = 3
VMEM_LIMIT_BYTES = 56 * 1024 * 1024
NEG_BIG = -1e30

_f32 = jnp.float32
_bf16 = jnp.bfloat16


def _cparams(sem):
    return pltpu.CompilerParams(dimension_semantics=sem, vmem_limit_bytes=VMEM_LIMIT_BYTES)


def _resident(shape):
    nd = len(shape)
    return pl.BlockSpec(shape, lambda *_: (0,) * nd, pipeline_mode=pl.Buffered(1))


def _rms_scale(x):
    return lax.rsqrt(jnp.mean(x * x, axis=-1, keepdims=True) + EPS)


def _split3(x):
    hi = x.astype(_bf16)
    r1 = x - hi.astype(_f32)
    mid = r1.astype(_bf16)
    lo = (r1 - mid.astype(_f32)).astype(_bf16)
    return hi, mid, lo


def _in_proj_kernel(x_ref, g_ref, wn_ref, wt_ref, bf_ref, u_ref, k_ref, lf_ref, qt_ref, vt_ref,
                    *, d_conv, d_attn):
    x = x_ref[...]
    h = (x * _rms_scale(x) * g_ref[...]).astype(_bf16)
    nat = jnp.dot(h, wn_ref[...], preferred_element_type=_f32)
    a_val = nat[:, :d_conv]
    a_gate = nat[:, d_conv:2 * d_conv]
    u_ref[...] = (a_val * jax.nn.sigmoid(a_gate)).astype(u_ref.dtype)
    k_ref[...] = nat[:, 2 * d_conv:2 * d_conv + d_attn].astype(k_ref.dtype)
    z = nat[:, 2 * d_conv + d_attn:] + bf_ref[...]
    lf_ref[...] = jnp.minimum(z, 0.0) - jnp.log1p(jnp.exp(-jnp.abs(z)))
    tr = lax.dot_general(wt_ref[...], h, (((1,), (1,)), ((), ())), preferred_element_type=_f32)
    qt_ref[...] = tr[:d_attn].astype(qt_ref.dtype)
    vt_ref[...] = tr[d_attn:].astype(vt_ref.dtype)


def _in_proj(x, g, w_nat, w_tr, bf_ext, *, d_conv, d_attn, tm):
    b, s, d = x.shape
    n_nat = w_nat.shape[1]
    kern = functools.partial(_in_proj_kernel, d_conv=d_conv, d_attn=d_attn)
    return pl.pallas_call(
        kern,
        grid=(b, s // tm),
        in_specs=[
            pl.BlockSpec((None, tm, d), lambda i, j: (i, j, 0)),
            _resident((1, d)),
            _resident((d, n_nat)),
            _resident((2 * d_attn, d)),
            _resident((1, LANES)),
        ],
        out_specs=[
            pl.BlockSpec((None, tm, d_conv), lambda i, j: (i, j, 0)),
            pl.BlockSpec((None, tm, d_attn), lambda i, j: (i, j, 0)),
            pl.BlockSpec((None, tm, LANES), lambda i, j: (i, j, 0)),
            pl.BlockSpec((None, d_attn, tm), lambda i, j: (i, 0, j)),
            pl.BlockSpec((None, d_attn, tm), lambda i, j: (i, 0, j)),
        ],
        out_shape=[
            jax.ShapeDtypeStruct((b, s, d_conv), _bf16),
            jax.ShapeDtypeStruct((b, s, d_attn), _bf16),
            jax.ShapeDtypeStruct((b, s, LANES), _f32),
            jax.ShapeDtypeStruct((b, d_attn, s), _bf16),
            jax.ShapeDtypeStruct((b, d_attn, s), _bf16),
        ],
        compiler_params=_cparams(("parallel", "parallel")),
        name="in_proj",
    )(x, g, w_nat, w_tr, bf_ext)


def _decay_kernel(lf_ref, aug_ref, *, n_groups, n_cols):
    row = lax.broadcasted_iota(jnp.int32, (LANES, LANES), 0)
    col = lax.broadcasted_iota(jnp.int32, (LANES, LANES), 1)
    tri = (col <= row).astype(_bf16)
    piece = col % DECAY_SPLITS

    def body(gi, carry):
        r0 = pl.multiple_of(gi * LANES, LANES)
        hi, mid, lo = _split3(lf_ref[pl.ds(r0, LANES), :])
        local = (jnp.dot(tri, hi, preferred_element_type=_f32)
                 + jnp.dot(tri, mid, preferred_element_type=_f32)
                 + jnp.dot(tri, lo, preferred_element_type=_f32))
        c = local + carry
        nhi, nmid, nlo = _split3(-c)
        aug = jnp.where(piece == 0, nhi, jnp.where(piece == 1, nmid, nlo))
        aug_ref[pl.ds(r0, LANES), :] = jnp.where(col < n_cols, aug, jnp.zeros_like(aug))
        return c[LANES - 1:LANES, :]

    lax.fori_loop(0, n_groups, body, jnp.zeros((1, LANES), _f32))


def _decay_columns(lf, *, n_cols):
    b, s, _ = lf.shape
    kern = functools.partial(_decay_kernel, n_groups=s // LANES, n_cols=n_cols)
    return pl.pallas_call(
        kern,
        grid=(b,),
        in_specs=[pl.BlockSpec((None, s, LANES), lambda i: (i, 0, 0))],
        out_specs=pl.BlockSpec((None, s, LANES), lambda i: (i, 0, 0)),
        out_shape=jax.ShapeDtypeStruct((b, s, LANES), _bf16),
        compiler_params=_cparams(("parallel",)),
        name="decay_columns",
    )(lf)


def _conv_kernel(cur_ref, halo_ref, w_ref, cb_ref, lg_ref, lb_ref, o_ref, win_ref, *, tc, chunk):
    first = pl.program_id(1) == 0
    halo = halo_ref[...].astype(_f32)
    win_ref[0, 0:HALO_ROWS, :] = jnp.where(first, jnp.zeros_like(halo), halo)
    win_ref[0, HALO_ROWS:, :] = cur_ref[...].astype(_f32)
    n_shift = tc + HALO_ROWS - SUBLANES
    for r in range(1, SUBLANES):
        win_ref[r, 0:n_shift, :] = win_ref[0, r:r + n_shift, :]
    lead = HALO_ROWS - (CONV_WIDTH - 1)

    def body(ci, _):
        r0 = pl.multiple_of(ci * chunk, chunk)
        acc = jnp.broadcast_to(cb_ref[...], (chunk, cb_ref.shape[1]))
        for k in range(CONV_WIDTH):
            phase, base = (lead + k) % SUBLANES, (lead + k) // SUBLANES * SUBLANES
            tap = win_ref[phase, pl.ds(pl.multiple_of(r0 + base, SUBLANES), chunk), :]
            acc = acc + w_ref[k:k + 1, :] * tap
        mu = jnp.mean(acc, axis=-1, keepdims=True)
        cen = acc - mu
        var = jnp.mean(cen * cen, axis=-1, keepdims=True)
        y = cen * lax.rsqrt(var + EPS) * lg_ref[...] + lb_ref[...]
        o_ref[pl.ds(r0, chunk), :] = (y * jax.nn.sigmoid(y)).astype(o_ref.dtype)
        return 0

    lax.fori_loop(0, tc // chunk, body, 0)


def _conv_module(u, conv_w, conv_b, ln_g, ln_b, *, tc, chunk):
    b, s, c = u.shape
    halo_blocks = tc // HALO_ROWS
    kern = functools.partial(_conv_kernel, tc=tc, chunk=chunk)
    return pl.pallas_call(
        kern,
        grid=(b, s // tc),
        in_specs=[
            pl.BlockSpec((None, tc, c), lambda i, j: (i, j, 0)),
            pl.BlockSpec((None, HALO_ROWS, c), lambda i, j: (i, jnp.maximum(j * halo_blocks - 1, 0), 0)),
            _resident((CONV_WIDTH, c)),
            _resident((1, c)),
            _resident((1, c)),
            _resident((1, c)),
        ],
        out_specs=pl.BlockSpec((None, tc, c), lambda i, j: (i, j, 0)),
        out_shape=jax.ShapeDtypeStruct((b, s, c), _bf16),
        scratch_shapes=[pltpu.VMEM((SUBLANES, tc + HALO_ROWS, c), _f32)],
        compiler_params=_cparams(("parallel", "parallel")),
        name="conv_module",
    )(u, u, conv_w, conv_b, ln_g, ln_b)


def _attn_kernel(qt_ref, k_ref, aug_ref, vt_ref, o_ref, q_ext_ref, s_ref, p_ref, acc_ref, m_ref,
                 alpha_ref, *, t):
    pair = pl.program_id(1)
    qi = pl.program_id(2)
    t0 = qi * t
    hd = HEAD_DIM
    acc_rows = acc_ref.shape[1]
    heads = range(2)

    row = lax.broadcasted_iota(jnp.int32, (LANES, t), 0)
    qt = qt_ref[...]
    for hh in heads:
        head = 2 * pair + hh
        top = jnp.where((row >= hh * hd) & (row < (hh + 1) * hd), qt, jnp.zeros_like(qt))
        sel = ((row >= DECAY_SPLITS * head) & (row < DECAY_SPLITS * (head + 1))).astype(_bf16)
        q_ext_ref[hh, 0:LANES, :] = top
        q_ext_ref[hh, LANES:, :] = sel
    acc_ref[...] = jnp.zeros_like(acc_ref)
    m_ref[...] = jnp.full_like(m_ref, NEG_BIG)
    alpha_ref[...] = jnp.ones_like(alpha_ref)
    p_ref[...] = jnp.zeros_like(p_ref)
    ones_rows = jnp.ones((acc_rows - hd, t), _bf16)

    def logits(s0):
        kk = jnp.concatenate([k_ref[pl.ds(s0, t), :], aug_ref[pl.ds(s0, t), :]], axis=1)
        return [jnp.dot(kk, q_ext_ref[hh], preferred_element_type=_f32) for hh in heads]

    def softmax_stage(hh, masked):
        def tile():
            st = s_ref[hh]
            if masked:
                valid = (lax.broadcasted_iota(jnp.int32, (t, t), 0)
                         <= lax.broadcasted_iota(jnp.int32, (t, t), 1))
                st = jnp.where(valid, st, NEG_BIG)
            return st
        m_old = m_ref[hh]
        m_new = jnp.maximum(m_old, jnp.max(tile(), axis=0, keepdims=True))
        m_ref[hh] = m_new
        return jnp.exp(tile() - m_new).astype(_bf16), jnp.exp(m_old - m_new)

    def weighted_values(hh, s0, p):
        v_ext = jnp.concatenate([vt_ref[hh * hd:(hh + 1) * hd, pl.ds(s0, t)], ones_rows], axis=0)
        return jnp.dot(v_ext, p, preferred_element_type=_f32)

    s_first = logits(0)
    for hh in heads:
        s_ref[hh] = s_first[hh]

    def body(j, _):
        s_next = logits(pl.multiple_of((j + 1) * t, t))
        s_prev = pl.multiple_of(jnp.maximum(j - 1, 0) * t, t)
        for hh in heads:
            p, alpha = softmax_stage(hh, masked=False)
            acc_ref[hh] = alpha_ref[hh] * acc_ref[hh] + weighted_values(hh, s_prev, p_ref[hh])
            alpha_ref[hh] = alpha
            p_ref[hh] = p
        for hh in heads:
            s_ref[hh] = s_next[hh]
        return 0

    lax.fori_loop(0, qi, body, 0)

    s_prev = pl.multiple_of(jnp.maximum(qi - 1, 0) * t, t)
    for hh in heads:
        p, alpha = softmax_stage(hh, masked=True)
        acc = alpha_ref[hh] * acc_ref[hh] + weighted_values(hh, s_prev, p_ref[hh])
        acc = alpha * acc + weighted_values(hh, pl.multiple_of(t0, t), p)
        inv_l = 1.0 / acc[hd:hd + 1, :]
        o_ref[hh * hd:(hh + 1) * hd, :] = (acc[:hd, :] * inv_l).astype(o_ref.dtype)


def _attention(qt, k, aug, vt, *, t):
    b, da, s = qt.shape
    n_pairs = da // LANES
    acc_rows = HEAD_DIM + BF16_ROWS
    kern = functools.partial(_attn_kernel, t=t)
    return pl.pallas_call(
        kern,
        grid=(b, n_pairs, s // t),
        in_specs=[
            pl.BlockSpec((None, LANES, t), lambda i, p, j: (i, p, j)),
            pl.BlockSpec((None, s, LANES), lambda i, p, j: (i, 0, p)),
            pl.BlockSpec((None, s, LANES), lambda i, p, j: (i, 0, 0)),
            pl.BlockSpec((None, LANES, s), lambda i, p, j: (i, p, 0)),
        ],
        out_specs=pl.BlockSpec((None, LANES, t), lambda i, p, j: (i, p, j)),
        out_shape=jax.ShapeDtypeStruct((b, da, s), _bf16),
        scratch_shapes=[
            pltpu.VMEM((2, 2 * LANES, t), _bf16),
            pltpu.VMEM((2, t, t), _f32),
            pltpu.VMEM((2, t, t), _bf16),
            pltpu.VMEM((2, acc_rows, t), _f32),
            pltpu.VMEM((2, 1, t), _f32),
            pltpu.VMEM((2, 1, t), _f32),
        ],
        compiler_params=_cparams(("parallel", "parallel", "arbitrary")),
        name="attention",
    )(qt, k, aug, vt)


def _mix_ffn_kernel(x_ref, conv_ref, at_ref, woc_ref, woa_ref, gpost_ref, gpre_ref, wg_ref, wu_ref,
                    wd_ref, gout_ref, o_ref):
    mix = jnp.dot(conv_ref[...], woc_ref[...], preferred_element_type=_f32)
    mix = mix + lax.dot_general(at_ref[...], woa_ref[...], (((0,), (0,)), ((), ())),
                                preferred_element_type=_f32)
    x1 = x_ref[...] + mix * _rms_scale(mix) * gpost_ref[...]
    h = (x1 * _rms_scale(x1) * gpre_ref[...]).astype(_bf16)
    gate = jnp.dot(h, wg_ref[...], preferred_element_type=_f32)
    up = jnp.dot(h, wu_ref[...], preferred_element_type=_f32)
    act = (gate * jax.nn.sigmoid(gate) * up).astype(_bf16)
    y = jnp.dot(act, wd_ref[...], preferred_element_type=_f32)
    o_ref[...] = x1 + y * _rms_scale(y) * gout_ref[...]


def _mix_ffn(x, conv_out, attn_t, wo_c, wo_a, g_post, g_pre, w_gate, w_up, w_down, g_out, *, tm):
    b, s, d = x.shape
    dc = conv_out.shape[2]
    da = attn_t.shape[1]
    dff = w_gate.shape[1]
    return pl.pallas_call(
        _mix_ffn_kernel,
        grid=(b, s // tm),
        in_specs=[
            pl.BlockSpec((None, tm, d), lambda i, j: (i, j, 0)),
            pl.BlockSpec((None, tm, dc), lambda i, j: (i, j, 0)),
            pl.BlockSpec((None, da, tm), lambda i, j: (i, 0, j)),
            _resident((dc, d)),
            _resident((da, d)),
            _resident((1, d)),
            _resident((1, d)),
            _resident((d, dff)),
            _resident((d, dff)),
            _resident((dff, d)),
            _resident((1, d)),
        ],
        out_specs=pl.BlockSpec((None, tm, d), lambda i, j: (i, j, 0)),
        out_shape=jax.ShapeDtypeStruct((b, s, d), x.dtype),
        compiler_params=_cparams(("parallel", "parallel")),
        name="mix_ffn",
    )(x, conv_out, attn_t, wo_c, wo_a, g_post, g_pre, w_gate, w_up, w_down, g_out)


def _layer(x, g_mix_pre, w_in, b_forget, conv_w, conv_b, conv_ln_g, conv_ln_b, w_out, g_mix_post,
           g_ffn_pre, w_gate, w_up, w_down, g_ffn_post):
    b, s, d = x.shape
    d_conv = conv_w.shape[1]
    n_heads = b_forget.shape[0]
    d_attn = n_heads * HEAD_DIM
    n_cols = DECAY_SPLITS * n_heads
    assert 2 * d_conv + 3 * d_attn + n_heads == w_in.shape[1]
    assert n_cols <= LANES and d_attn % LANES == 0

    tm = min(512, s)
    t_attn = min(512, s)
    assert s % tm == 0 and s % t_attn == 0 and s % LANES == 0

    o_q, o_k, o_v, o_f = 2 * d_conv, 2 * d_conv + d_attn, 2 * d_conv + 2 * d_attn, 2 * d_conv + 3 * d_attn
    w_f = jnp.repeat(w_in[:, o_f:], DECAY_SPLITS, axis=1)
    w_f = jnp.pad(w_f, ((0, 0), (0, LANES - n_cols)))
    w_nat = jnp.concatenate([w_in[:, :o_q], w_in[:, o_k:o_v], w_f], axis=1).astype(_bf16)
    w_tr = jnp.concatenate([w_in[:, o_q:o_k] * (HEAD_DIM ** -0.5), w_in[:, o_v:o_f]], axis=1).T.astype(_bf16)
    bf_ext = jnp.pad(jnp.repeat(b_forget.astype(_f32), DECAY_SPLITS), (0, LANES - n_cols))[None, :]

    row = lambda v: v.astype(_f32)[None, :]
    u, k, lf, qt, vt = _in_proj(x, row(g_mix_pre), w_nat, w_tr, bf_ext, d_conv=d_conv, d_attn=d_attn, tm=tm)
    aug = _decay_columns(lf, n_cols=n_cols)
    conv_out = _conv_module(u, conv_w.astype(_f32), row(conv_b), row(conv_ln_g), row(conv_ln_b),
                            tc=tm, chunk=32)
    attn_t = _attention(qt, k, aug, vt, t=t_attn)
    return _mix_ffn(x, conv_out, attn_t, w_out[:d_conv].astype(_bf16), w_out[d_conv:].astype(_bf16),
                    row(g_mix_post), row(g_ffn_pre), w_gate.astype(_bf16), w_up.astype(_bf16),
                    w_down.astype(_bf16), row(g_ffn_post), tm=tm)


def kernel(x, g_mix_pre, w_in, b_forget, conv_w, conv_b, conv_ln_g, conv_ln_b, w_out, g_mix_post,
           g_ffn_pre, w_gate, w_up, w_down, g_ffn_post):
    for l in range(g_mix_pre.shape[0]):
        x = _layer(x, g_mix_pre[l], w_in[l], b_forget[l], conv_w[l], conv_b[l], conv_ln_g[l],
                   conv_ln_b[l], w_out[l], g_mix_post[l], g_ffn_pre[l], w_gate[l], w_up[l],
                   w_down[l], g_ffn_post[l])
    return x
```

```python
import functools

import jax
import jax.numpy as jnp
from jax import lax
from jax.experimental import pallas as pl
from jax.experimental.pallas import tpu as pltpu

HEAD_DIM = 64
CONV_WIDTH = 31
EPS = 1e-6
LANES = 128
SUBLANES = 8
BF16_ROWS = 16
HALO_ROWS = 32
DECAY_SPLITS = 3
VMEM_LIMIT_BYTES = 56 * 1024 * 1024
NEG_BIG = -1e30
LOG2_E = 1.4426950408889634

_f32 = jnp.float32
_bf16 = jnp.bfloat16


def _cparams(sem):
    return pltpu.CompilerParams(dimension_semantics=sem, vmem_limit_bytes=VMEM_LIMIT_BYTES)


def _resident(shape):
    nd = len(shape)
    return pl.BlockSpec(shape, lambda *_: (0,) * nd, pipeline_mode=pl.Buffered(1))


def _rms_scale(x):
    return lax.rsqrt(jnp.mean(x * x, axis=-1, keepdims=True) + EPS)


def _split3(x):
    hi = x.astype(_bf16)
    r1 = x - hi.astype(_f32)
    mid = r1.astype(_bf16)
    lo = (r1 - mid.astype(_f32)).astype(_bf16)
    return hi, mid, lo


def _in_proj_kernel(x_ref, g_ref, wn_ref, wt_ref, bf_ref, u_ref, k_ref, lf_ref, qt_ref, vt_ref,
                    *, d_conv, d_attn):
    x = x_ref[...]
    h = (x * _rms_scale(x) * g_ref[...]).astype(_bf16)
    nat = jnp.dot(h, wn_ref[...], preferred_element_type=_f32)
    a_val = nat[:, :d_conv]
    a_gate = nat[:, d_conv:2 * d_conv]
    u_ref[...] = (a_val * jax.nn.sigmoid(a_gate)).astype(u_ref.dtype)
    k_ref[...] = nat[:, 2 * d_conv:2 * d_conv + d_attn].astype(k_ref.dtype)
    z = nat[:, 2 * d_conv + d_attn:] + bf_ref[...]
    log_f = jnp.minimum(z, 0.0) - jnp.log1p(jnp.exp(-jnp.abs(z)))
    lf_ref[...] = log_f * LOG2_E
    tr = lax.dot_general(wt_ref[...], h, (((1,), (1,)), ((), ())), preferred_element_type=_f32)
    qt_ref[...] = tr[:d_attn].astype(qt_ref.dtype)
    vt_ref[...] = tr[d_attn:].astype(vt_ref.dtype)


def _in_proj(x, g, w_nat, w_tr, bf_ext, *, d_conv, d_attn, tm):
    b, s, d = x.shape
    n_nat = w_nat.shape[1]
    kern = functools.partial(_in_proj_kernel, d_conv=d_conv, d_attn=d_attn)
    return pl.pallas_call(
        kern,
        grid=(b, s // tm),
        in_specs=[
            pl.BlockSpec((None, tm, d), lambda i, j: (i, j, 0)),
            _resident((1, d)),
            _resident((d, n_nat)),
            _resident((2 * d_attn, d)),
            _resident((1, LANES)),
        ],
        out_specs=[
            pl.BlockSpec((None, tm, d_conv), lambda i, j: (i, j, 0)),
            pl.BlockSpec((None, tm, d_attn), lambda i, j: (i, j, 0)),
            pl.BlockSpec((None, tm, LANES), lambda i, j: (i, j, 0)),
            pl.BlockSpec((None, d_attn, tm), lambda i, j: (i, 0, j)),
            pl.BlockSpec((None, d_attn, tm), lambda i, j: (i, 0, j)),
        ],
        out_shape=[
            jax.ShapeDtypeStruct((b, s, d_conv), _bf16),
            jax.ShapeDtypeStruct((b, s, d_attn), _bf16),
            jax.ShapeDtypeStruct((b, s, LANES), _f32),
            jax.ShapeDtypeStruct((b, d_attn, s), _bf16),
            jax.ShapeDtypeStruct((b, d_attn, s), _bf16),
        ],
        compiler_params=_cparams(("parallel", "parallel")),
        name="in_proj",
    )(x, g, w_nat, w_tr, bf_ext)


def _decay_kernel(lf_ref, aug_ref, *, n_groups, n_cols):
    row = lax.broadcasted_iota(jnp.int32, (LANES, LANES), 0)
    col = lax.broadcasted_iota(jnp.int32, (LANES, LANES), 1)
    tri = (col <= row).astype(_bf16)
    piece = col % DECAY_SPLITS

    def body(gi, carry):
        r0 = pl.multiple_of(gi * LANES, LANES)
        hi, mid, lo = _split3(lf_ref[pl.ds(r0, LANES), :])
        local = (jnp.dot(tri, hi, preferred_element_type=_f32)
                 + jnp.dot(tri, mid, preferred_element_type=_f32)
                 + jnp.dot(tri, lo, preferred_element_type=_f32))
        c = local + carry
        nhi, nmid, nlo = _split3(-c)
        aug = jnp.where(piece == 0, nhi, jnp.where(piece == 1, nmid, nlo))
        aug_ref[pl.ds(r0, LANES), :] = jnp.where(col < n_cols, aug, jnp.zeros_like(aug))
        return c[LANES - 1:LANES, :]

    lax.fori_loop(0, n_groups, body, jnp.zeros((1, LANES), _f32))


def _decay_columns(lf, *, n_cols):
    b, s, _ = lf.shape
    kern = functools.partial(_decay_kernel, n_groups=s // LANES, n_cols=n_cols)
    return pl.pallas_call(
        kern,
        grid=(b,),
        in_specs=[pl.BlockSpec((None, s, LANES), lambda i: (i, 0, 0))],
        out_specs=pl.BlockSpec((None, s, LANES), lambda i: (i, 0, 0)),
        out_shape=jax.ShapeDtypeStruct((b, s, LANES), _bf16),
        compiler_params=_cparams(("parallel",)),
        name="decay_columns",
    )(lf)


def _conv_kernel(cur_ref, halo_ref, w_ref, cb_ref, lg_ref, lb_ref, o_ref, win_ref, pre_ref, *, tc, chunk):
    first = pl.program_id(1) == 0
    halo = halo_ref[...].astype(_f32)
    win_ref[0, 0:HALO_ROWS, :] = jnp.where(first, jnp.zeros_like(halo), halo)
    win_ref[0, HALO_ROWS:, :] = cur_ref[...].astype(_f32)
    n_shift = tc + HALO_ROWS - SUBLANES
    for r in range(1, SUBLANES):
        win_ref[r, 0:n_shift, :] = win_ref[0, r:r + n_shift, :]
    lead = HALO_ROWS - (CONV_WIDTH - 1)

    def body(ci, _):
        r0 = pl.multiple_of(ci * chunk, chunk)
        acc = jnp.broadcast_to(cb_ref[...], (chunk, cb_ref.shape[1]))
        for k in range(CONV_WIDTH):
            phase, base = (lead + k) % SUBLANES, (lead + k) // SUBLANES * SUBLANES
            tap = win_ref[phase, pl.ds(pl.multiple_of(r0 + base, SUBLANES), chunk), :]
            acc = acc + w_ref[k:k + 1, :] * tap
        pre_ref[pl.ds(r0, chunk), :] = acc
        return 0

    lax.fori_loop(0, tc // chunk, body, 0)

    pre = pre_ref[...]
    mu = jnp.mean(pre, axis=-1, keepdims=True)
    cen = pre - mu
    var = jnp.mean(cen * cen, axis=-1, keepdims=True)
    y = cen * lax.rsqrt(var + EPS) * lg_ref[...] + lb_ref[...]
    o_ref[...] = (y * jax.nn.sigmoid(y)).astype(o_ref.dtype)


def _conv_module(u, conv_w, conv_b, ln_g, ln_b, *, tc, chunk):
    b, s, c = u.shape
    halo_blocks = tc // HALO_ROWS
    kern = functools.partial(_conv_kernel, tc=tc, chunk=chunk)
    return pl.pallas_call(
        kern,
        grid=(b, s // tc),
        in_specs=[
            pl.BlockSpec((None, tc, c), lambda i, j: (i, j, 0)),
            pl.BlockSpec((None, HALO_ROWS, c), lambda i, j: (i, jnp.maximum(j * halo_blocks - 1, 0), 0)),
            _resident((CONV_WIDTH, c)),
            _resident((1, c)),
            _resident((1, c)),
            _resident((1, c)),
        ],
        out_specs=pl.BlockSpec((None, tc, c), lambda i, j: (i, j, 0)),
        out_shape=jax.ShapeDtypeStruct((b, s, c), _bf16),
        scratch_shapes=[pltpu.VMEM((SUBLANES, tc + HALO_ROWS, c), _f32), pltpu.VMEM((tc, c), _f32)],
        compiler_params=_cparams(("parallel", "parallel")),
        name="conv_module",
    )(u, u, conv_w, conv_b, ln_g, ln_b)


def _attn_kernel(qt_ref, k_ref, aug_ref, vt_ref, o_ref, q_ext_ref, s_ref, mx_ref, p_ref, acc_ref, m_ref,
                 alpha_ref, *, t):
    group = pl.program_id(1)
    qi = pl.program_id(2)
    t0 = qi * t
    hd = HEAD_DIM
    acc_rows = acc_ref.shape[1]
    n_heads = acc_ref.shape[0]
    heads = range(n_heads)

    row = lax.broadcasted_iota(jnp.int32, (LANES, t), 0)
    for hh in heads:
        head = n_heads * group + hh
        lo = (hh % 2) * hd
        qt = qt_ref[(hh // 2) * LANES:(hh // 2 + 1) * LANES, :]
        top = jnp.where((row >= lo) & (row < lo + hd), qt, jnp.zeros_like(qt))
        sel = ((row >= DECAY_SPLITS * head) & (row < DECAY_SPLITS * (head + 1))).astype(_bf16)
        q_ext_ref[hh, 0:LANES, :] = top
        q_ext_ref[hh, LANES:, :] = sel
    acc_ref[...] = jnp.zeros_like(acc_ref)
    m_ref[...] = jnp.full_like(m_ref, NEG_BIG)
    alpha_ref[...] = jnp.ones_like(alpha_ref)
    p_ref[1] = jnp.zeros_like(p_ref[1])
    ones_rows = jnp.ones((acc_rows - hd, t), _bf16)

    def logits_into(slot, s0):
        aug = aug_ref[pl.ds(s0, t), :]
        for hh in heads:
            k_pair = k_ref[pl.ds(s0, t), (hh // 2) * LANES:(hh // 2 + 1) * LANES]
            kk = jnp.concatenate([k_pair, aug], axis=1)
            st = jnp.dot(kk, q_ext_ref[hh], preferred_element_type=_f32)
            s_ref[slot, hh] = st
            mx_ref[slot, hh] = jnp.max(st, axis=0, keepdims=True)

    def softmax_stage(slot, hh, masked):
        def tile():
            st = s_ref[slot, hh]
            if masked:
                valid = (lax.broadcasted_iota(jnp.int32, (t, t), 0)
                         <= lax.broadcasted_iota(jnp.int32, (t, t), 1))
                st = jnp.where(valid, st, NEG_BIG)
            return st
        tile_max = jnp.max(tile(), axis=0, keepdims=True) if masked else mx_ref[slot, hh]
        m_old = m_ref[hh]
        m_new = jnp.maximum(m_old, tile_max)
        m_ref[hh] = m_new
        return jnp.exp2(tile() - m_new).astype(_bf16), jnp.exp2(m_old - m_new)

    def weighted_values(hh, s0, p):
        v_ext = jnp.concatenate([vt_ref[hh * hd:(hh + 1) * hd, pl.ds(s0, t)], ones_rows], axis=0)
        return jnp.dot(v_ext, p, preferred_element_type=_f32)

    def trip(j, slot):
        logits_into(1 - slot, pl.multiple_of((j + 1) * t, t))
        s_prev = pl.multiple_of(jnp.maximum(j - 1, 0) * t, t)
        for hh in heads:
            p, alpha = softmax_stage(slot, hh, masked=False)
            acc_ref[hh] = alpha_ref[hh] * acc_ref[hh] + weighted_values(hh, s_prev, p_ref[1 - slot, hh])
            alpha_ref[hh] = alpha
            p_ref[slot, hh] = p

    def finish(slot):
        s_prev = pl.multiple_of(jnp.maximum(qi - 1, 0) * t, t)
        for hh in heads:
            p, alpha = softmax_stage(slot, hh, masked=True)
            acc = alpha_ref[hh] * acc_ref[hh] + weighted_values(hh, s_prev, p_ref[1 - slot, hh])
            acc = alpha * acc + weighted_values(hh, pl.multiple_of(t0, t), p)
            inv_l = 1.0 / acc[hd:hd + 1, :]
            o_ref[hh * hd:(hh + 1) * hd, :] = (acc[:hd, :] * inv_l).astype(o_ref.dtype)

    logits_into(0, 0)

    def two_trips(i, _):
        trip(2 * i, 0)
        trip(2 * i + 1, 1)
        return 0

    lax.fori_loop(0, lax.shift_right_logical(qi, 1), two_trips, 0)
    odd = (qi & 1) == 1

    @pl.when(odd)
    def _():
        trip(qi - 1, 0)
        finish(1)

    @pl.when(jnp.logical_not(odd))
    def _():
        finish(0)


def _attention(qt, k, aug, vt, *, t, hps):
    b, da, s = qt.shape
    w = hps * HEAD_DIM
    acc_rows = HEAD_DIM + BF16_ROWS
    kern = functools.partial(_attn_kernel, t=t)
    return pl.pallas_call(
        kern,
        grid=(b, da // w, s // t),
        in_specs=[
            pl.BlockSpec((None, w, t), lambda i, g, j: (i, g, j)),
            pl.BlockSpec((None, s, w), lambda i, g, j: (i, 0, g), pipeline_mode=pl.Buffered(1)),
            pl.BlockSpec((None, s, LANES), lambda i, g, j: (i, 0, 0), pipeline_mode=pl.Buffered(1)),
            pl.BlockSpec((None, w, s), lambda i, g, j: (i, g, 0), pipeline_mode=pl.Buffered(1)),
        ],
        out_specs=pl.BlockSpec((None, w, t), lambda i, g, j: (i, g, j)),
        out_shape=jax.ShapeDtypeStruct((b, da, s), _bf16),
        scratch_shapes=[
            pltpu.VMEM((hps, 2 * LANES, t), _bf16),
            pltpu.VMEM((2, hps, t, t), _f32),
            pltpu.VMEM((2, hps, 1, t), _f32),
            pltpu.VMEM((2, hps, t, t), _bf16),
            pltpu.VMEM((hps, acc_rows, t), _f32),
            pltpu.VMEM((hps, 1, t), _f32),
            pltpu.VMEM((hps, 1, t), _f32),
        ],
        compiler_params=_cparams(("parallel", "parallel", "arbitrary")),
        name="attention",
    )(qt, k, aug, vt)


def _mix_ffn_kernel(x_ref, conv_ref, at_ref, woc_ref, woa_ref, gpost_ref, gpre_ref, wg_ref, wu_ref,
                    wd_ref, gout_ref, o_ref):
    mix = jnp.dot(conv_ref[...], woc_ref[...], preferred_element_type=_f32)
    mix = mix + lax.dot_general(at_ref[...], woa_ref[...], (((0,), (0,)), ((), ())),
                                preferred_element_type=_f32)
    x1 = x_ref[...] + mix * _rms_scale(mix) * gpost_ref[...]
    h = (x1 * _rms_scale(x1) * gpre_ref[...]).astype(_bf16)
    gate = jnp.dot(h, wg_ref[...], preferred_element_type=_f32)
    up = jnp.dot(h, wu_ref[...], preferred_element_type=_f32)
    act = (gate * jax.nn.sigmoid(gate) * up).astype(_bf16)
    y = jnp.dot(act, wd_ref[...], preferred_element_type=_f32)
    o_ref[...] = x1 + y * _rms_scale(y) * gout_ref[...]


def _mix_ffn(x, conv_out, attn_t, wo_c, wo_a, g_post, g_pre, w_gate, w_up, w_down, g_out, *, tm):
    b, s, d = x.shape
    dc = conv_out.shape[2]
    da = attn_t.shape[1]
    dff = w_gate.shape[1]
    return pl.pallas_call(
        _mix_ffn_kernel,
        grid=(b, s // tm),
        in_specs=[
            pl.BlockSpec((None, tm, d), lambda i, j: (i, j, 0)),
            pl.BlockSpec((None, tm, dc), lambda i, j: (i, j, 0)),
            pl.BlockSpec((None, da, tm), lambda i, j: (i, 0, j)),
            _resident((dc, d)),
            _resident((da, d)),
            _resident((1, d)),
            _resident((1, d)),
            _resident((d, dff)),
            _resident((d, dff)),
            _resident((dff, d)),
            _resident((1, d)),
        ],
        out_specs=pl.BlockSpec((None, tm, d), lambda i, j: (i, j, 0)),
        out_shape=jax.ShapeDtypeStruct((b, s, d), x.dtype),
        compiler_params=_cparams(("parallel", "parallel")),
        name="mix_ffn",
    )(x, conv_out, attn_t, wo_c, wo_a, g_post, g_pre, w_gate, w_up, w_down, g_out)


def _layer(x, g_mix_pre, w_in, b_forget, conv_w, conv_b, conv_ln_g, conv_ln_b, w_out, g_mix_post,
           g_ffn_pre, w_gate, w_up, w_down, g_ffn_post):
    b, s, d = x.shape
    d_conv = conv_w.shape[1]
    n_heads = b_forget.shape[0]
    d_attn = n_heads * HEAD_DIM
    n_cols = DECAY_SPLITS * n_heads
    assert 2 * d_conv + 3 * d_attn + n_heads == w_in.shape[1]
    assert n_cols <= LANES and d_attn % LANES == 0

    tm = min(512, s)
    t_attn = min(512, s)
    assert s % tm == 0 and s % t_attn == 0 and s % LANES == 0

    o_q, o_k, o_v, o_f = 2 * d_conv, 2 * d_conv + d_attn, 2 * d_conv + 2 * d_attn, 2 * d_conv + 3 * d_attn
    w_f = jnp.repeat(w_in[:, o_f:], DECAY_SPLITS, axis=1)
    w_f = jnp.pad(w_f, ((0, 0), (0, LANES - n_cols)))
    w_nat = jnp.concatenate([w_in[:, :o_q], w_in[:, o_k:o_v], w_f], axis=1).astype(_bf16)
    w_tr = jnp.concatenate([w_in[:, o_q:o_k] * (HEAD_DIM ** -0.5 * LOG2_E), w_in[:, o_v:o_f]], axis=1).T.astype(_bf16)
    bf_ext = jnp.pad(jnp.repeat(b_forget.astype(_f32), DECAY_SPLITS), (0, LANES - n_cols))[None, :]

    row = lambda v: v.astype(_f32)[None, :]
    u, k, lf, qt, vt = _in_proj(x, row(g_mix_pre), w_nat, w_tr, bf_ext, d_conv=d_conv, d_attn=d_attn, tm=tm)
    aug = _decay_columns(lf, n_cols=n_cols)
    conv_out = _conv_module(u, conv_w.astype(_f32), row(conv_b), row(conv_ln_g), row(conv_ln_b),
                            tc=tm, chunk=32)
    attn_t = _attention(qt, k, aug, vt, t=t_attn, hps=8)
    return _mix_ffn(x, conv_out, attn_t, w_out[:d_conv].astype(_bf16), w_out[d_conv:].astype(_bf16),
                    row(g_mix_post), row(g_ffn_pre), w_gate.astype(_bf16), w_up.astype(_bf16),
                    w_down.astype(_bf16), row(g_ffn_post), tm=tm)


def kernel(x, g_mix_pre, w_in, b_forget, conv_w, conv_b, conv_ln_g, conv_ln_b, w_out, g_mix_post,
           g_ffn_pre, w_gate, w_up, w_down, g_ffn_post):
    for l in range(g_mix_pre.shape[0]):
        x = _layer(x, g_mix_pre[l], w_in[l], b_forget[l], conv_w[l], conv_b[l], conv_ln_g[l],
                   conv_ln_b[l], w_out[l], g_mix_post[l], g_ffn_pre[l], w_gate[l], w_up[l],
                   w_down[l], g_ffn_post[l])
    return x
```

```python
import functools

import jax
import jax.numpy as jnp
from jax import lax
from jax.experimental import pallas as pl
from jax.experimental.pallas import tpu as pltpu

HEAD_DIM = 64
CONV_WIDTH = 31
EPS = 1e-6
LANES = 128
SUBLANES = 8
BF16_ROWS = 16
HALO_ROWS = 32
CONV_CHUNK = 32
DECAY_SPLITS = 3
VMEM_LIMIT_BYTES = 56 * 1024 * 1024
NEG_BIG = -1e30
LOG2_E = 1.4426950408889634

_f32 = jnp.float32
_bf16 = jnp.bfloat16


def _cparams(sem):
    return pltpu.CompilerParams(dimension_semantics=sem, vmem_limit_bytes=VMEM_LIMIT_BYTES)


def _resident(shape):
    nd = len(shape)
    return pl.BlockSpec(shape, lambda *_: (0,) * nd, pipeline_mode=pl.Buffered(1))


def _rms_scale(x):
    return lax.rsqrt(jnp.mean(x * x, axis=-1, keepdims=True) + EPS)


def _split3(x):
    hi = x.astype(_bf16)
    r1 = x - hi.astype(_f32)
    mid = r1.astype(_bf16)
    lo = (r1 - mid.astype(_f32)).astype(_bf16)
    return hi, mid, lo


def _in_proj_kernel(x_ref, g_ref, wn_ref, wt_ref, bf_ref, u_ref, k_ref, aug_ref, qt_ref, vt_ref,
                    carry_ref, *, d_conv, d_attn, n_cols):
    @pl.when(pl.program_id(1) == 0)
    def _():
        carry_ref[...] = jnp.zeros_like(carry_ref)

    x = x_ref[...]
    tm = x.shape[0]
    h = (x * _rms_scale(x) * g_ref[...]).astype(_bf16)
    nat = jnp.dot(h, wn_ref[...], preferred_element_type=_f32)
    a_val = nat[:, :d_conv]
    a_gate = nat[:, d_conv:2 * d_conv]
    u_ref[...] = (a_val * jax.nn.sigmoid(a_gate)).astype(u_ref.dtype)
    k_ref[...] = nat[:, 2 * d_conv:2 * d_conv + d_attn].astype(k_ref.dtype)
    tr = lax.dot_general(wt_ref[...], h, (((1,), (1,)), ((), ())), preferred_element_type=_f32)
    qt_ref[...] = tr[:d_attn].astype(qt_ref.dtype)
    vt_ref[...] = tr[d_attn:].astype(vt_ref.dtype)

    z = nat[:, 2 * d_conv + d_attn:] + bf_ref[...]
    log_f = jnp.minimum(z, 0.0) - jnp.log1p(jnp.exp(-jnp.abs(z)))
    lf = log_f * LOG2_E
    row = lax.broadcasted_iota(jnp.int32, (LANES, LANES), 0)
    col = lax.broadcasted_iota(jnp.int32, (LANES, LANES), 1)
    tri = (col <= row).astype(_bf16)
    piece = col % DECAY_SPLITS
    local = []
    for gi in range(tm // LANES):
        hi, mid, lo = _split3(lf[gi * LANES:(gi + 1) * LANES, :])
        local.append(jnp.dot(tri, hi, preferred_element_type=_f32)
                     + jnp.dot(tri, mid, preferred_element_type=_f32)
                     + jnp.dot(tri, lo, preferred_element_type=_f32))
    carry = carry_ref[...]
    for gi in range(tm // LANES):
        c = local[gi] + carry
        nhi, nmid, nlo = _split3(-c)
        aug = jnp.where(piece == 0, nhi, jnp.where(piece == 1, nmid, nlo))
        aug_ref[gi * LANES:(gi + 1) * LANES, :] = jnp.where(col < n_cols, aug, jnp.zeros_like(aug))
        carry = c[LANES - 1:LANES, :]
    carry_ref[...] = carry


def _in_proj(x, g, w_nat, w_tr, bf_ext, *, d_conv, d_attn, n_cols, tm):
    b, s, d = x.shape
    n_nat = w_nat.shape[1]
    kern = functools.partial(_in_proj_kernel, d_conv=d_conv, d_attn=d_attn, n_cols=n_cols)
    return pl.pallas_call(
        kern,
        grid=(b, s // tm),
        in_specs=[
            pl.BlockSpec((None, tm, d), lambda i, j: (i, j, 0)),
            _resident((1, d)),
            _resident((d, n_nat)),
            _resident((2 * d_attn, d)),
            _resident((1, LANES)),
        ],
        out_specs=[
            pl.BlockSpec((None, tm, d_conv), lambda i, j: (i, j, 0)),
            pl.BlockSpec((None, tm, d_attn), lambda i, j: (i, j, 0)),
            pl.BlockSpec((None, tm, LANES), lambda i, j: (i, j, 0)),
            pl.BlockSpec((None, d_attn, tm), lambda i, j: (i, 0, j)),
            pl.BlockSpec((None, d_attn, tm), lambda i, j: (i, 0, j)),
        ],
        out_shape=[
            jax.ShapeDtypeStruct((b, s, d_conv), _bf16),
            jax.ShapeDtypeStruct((b, s, d_attn), _bf16),
            jax.ShapeDtypeStruct((b, s, LANES), _bf16),
            jax.ShapeDtypeStruct((b, d_attn, s), _bf16),
            jax.ShapeDtypeStruct((b, d_attn, s), _bf16),
        ],
        scratch_shapes=[pltpu.VMEM((1, LANES), _f32)],
        compiler_params=_cparams(("parallel", "arbitrary")),
        name="in_proj",
    )(x, g, w_nat, w_tr, bf_ext)


def _attn_kernel(qt_ref, k_ref, aug_ref, vt_ref, o_ref, q_ext_ref, s_ref, mx_ref, p_ref, acc_ref, m_ref,
                 alpha_ref, *, t):
    group = pl.program_id(1)
    qi = pl.program_id(2)
    t0 = qi * t
    hd = HEAD_DIM
    acc_rows = acc_ref.shape[1]
    n_heads = acc_ref.shape[0]
    heads = range(n_heads)

    row = lax.broadcasted_iota(jnp.int32, (LANES, t), 0)
    for hh in heads:
        head = n_heads * group + hh
        lo = (hh % 2) * hd
        qt = qt_ref[(hh // 2) * LANES:(hh // 2 + 1) * LANES, :]
        top = jnp.where((row >= lo) & (row < lo + hd), qt, jnp.zeros_like(qt))
        sel = ((row >= DECAY_SPLITS * head) & (row < DECAY_SPLITS * (head + 1))).astype(_bf16)
        q_ext_ref[hh, 0:LANES, :] = top
        q_ext_ref[hh, LANES:, :] = sel
    acc_ref[...] = jnp.zeros_like(acc_ref)
    m_ref[...] = jnp.full_like(m_ref, NEG_BIG)
    alpha_ref[...] = jnp.ones_like(alpha_ref)
    p_ref[1] = jnp.zeros_like(p_ref[1])
    ones_rows = jnp.ones((acc_rows - hd, t), _bf16)

    def logits_into(slot, s0):
        aug = aug_ref[pl.ds(s0, t), :]
        for hh in heads:
            k_pair = k_ref[pl.ds(s0, t), (hh // 2) * LANES:(hh // 2 + 1) * LANES]
            kk = jnp.concatenate([k_pair, aug], axis=1)
            st = jnp.dot(kk, q_ext_ref[hh], preferred_element_type=_f32)
            s_ref[slot, hh] = st
            mx_ref[slot, hh] = jnp.max(st, axis=0, keepdims=True)

    def softmax_stage(slot, hh, masked):
        def tile():
            st = s_ref[slot, hh]
            if masked:
                valid = (lax.broadcasted_iota(jnp.int32, (t, t), 0)
                         <= lax.broadcasted_iota(jnp.int32, (t, t), 1))
                st = jnp.where(valid, st, NEG_BIG)
            return st
        tile_max = jnp.max(tile(), axis=0, keepdims=True) if masked else mx_ref[slot, hh]
        m_old = m_ref[hh]
        m_new = jnp.maximum(m_old, tile_max)
        m_ref[hh] = m_new
        return jnp.exp2(tile() - m_new).astype(_bf16), jnp.exp2(m_old - m_new)

    def weighted_values(hh, s0, p):
        v_ext = jnp.concatenate([vt_ref[hh * hd:(hh + 1) * hd, pl.ds(s0, t)], ones_rows], axis=0)
        return jnp.dot(v_ext, p, preferred_element_type=_f32)

    def trip(j, slot):
        logits_into(1 - slot, pl.multiple_of((j + 1) * t, t))
        s_prev = pl.multiple_of(jnp.maximum(j - 1, 0) * t, t)
        for hh in heads:
            p, alpha = softmax_stage(slot, hh, masked=False)
            acc_ref[hh] = alpha_ref[hh] * acc_ref[hh] + weighted_values(hh, s_prev, p_ref[1 - slot, hh])
            alpha_ref[hh] = alpha
            p_ref[slot, hh] = p

    def finish(slot):
        s_prev = pl.multiple_of(jnp.maximum(qi - 1, 0) * t, t)
        for hh in heads:
            p, alpha = softmax_stage(slot, hh, masked=True)
            acc = alpha_ref[hh] * acc_ref[hh] + weighted_values(hh, s_prev, p_ref[1 - slot, hh])
            acc = alpha * acc + weighted_values(hh, pl.multiple_of(t0, t), p)
            inv_l = 1.0 / acc[hd:hd + 1, :]
            o_ref[hh * hd:(hh + 1) * hd, :] = (acc[:hd, :] * inv_l).astype(o_ref.dtype)

    logits_into(0, 0)

    def two_trips(i, _):
        trip(2 * i, 0)
        trip(2 * i + 1, 1)
        return 0

    lax.fori_loop(0, lax.shift_right_logical(qi, 1), two_trips, 0)
    odd = (qi & 1) == 1

    @pl.when(odd)
    def _():
        trip(qi - 1, 0)
        finish(1)

    @pl.when(jnp.logical_not(odd))
    def _():
        finish(0)


def _attention(qt, k, aug, vt, *, t, hps):
    b, da, s = qt.shape
    w = hps * HEAD_DIM
    acc_rows = HEAD_DIM + BF16_ROWS
    kern = functools.partial(_attn_kernel, t=t)
    return pl.pallas_call(
        kern,
        grid=(b, da // w, s // t),
        in_specs=[
            pl.BlockSpec((None, w, t), lambda i, g, j: (i, g, j)),
            pl.BlockSpec((None, s, w), lambda i, g, j: (i, 0, g), pipeline_mode=pl.Buffered(1)),
            pl.BlockSpec((None, s, LANES), lambda i, g, j: (i, 0, 0), pipeline_mode=pl.Buffered(1)),
            pl.BlockSpec((None, w, s), lambda i, g, j: (i, g, 0), pipeline_mode=pl.Buffered(1)),
        ],
        out_specs=pl.BlockSpec((None, w, t), lambda i, g, j: (i, g, j)),
        out_shape=jax.ShapeDtypeStruct((b, da, s), _bf16),
        scratch_shapes=[
            pltpu.VMEM((hps, 2 * LANES, t), _bf16),
            pltpu.VMEM((2, hps, t, t), _f32),
            pltpu.VMEM((2, hps, 1, t), _f32),
            pltpu.VMEM((2, hps, t, t), _bf16),
            pltpu.VMEM((hps, acc_rows, t), _f32),
            pltpu.VMEM((hps, 1, t), _f32),
            pltpu.VMEM((hps, 1, t), _f32),
        ],
        compiler_params=_cparams(("parallel", "parallel", "arbitrary")),
        name="attention",
    )(qt, k, aug, vt)


def _conv_tile(u_cur, halo, w_ref, cb_ref, lg_ref, lb_ref, win_ref, out_ref):
    tc = u_cur.shape[0]
    win_ref[0, 0:HALO_ROWS, :] = halo
    win_ref[0, HALO_ROWS:, :] = u_cur.astype(_f32)
    n_shift = tc + HALO_ROWS - SUBLANES
    for r in range(1, SUBLANES):
        win_ref[r, 0:n_shift, :] = win_ref[0, r:r + n_shift, :]
    lead = HALO_ROWS - (CONV_WIDTH - 1)
    for r0 in range(0, tc, CONV_CHUNK):
        acc = jnp.broadcast_to(cb_ref[...], (CONV_CHUNK, cb_ref.shape[1]))
        for k in range(CONV_WIDTH):
            phase, base = (lead + k) % SUBLANES, (lead + k) // SUBLANES * SUBLANES
            acc = acc + w_ref[k:k + 1, :] * win_ref[phase, r0 + base:r0 + base + CONV_CHUNK, :]
        mu = jnp.mean(acc, axis=-1, keepdims=True)
        cen = acc - mu
        var = jnp.mean(cen * cen, axis=-1, keepdims=True)
        y = cen * lax.rsqrt(var + EPS) * lg_ref[...] + lb_ref[...]
        out_ref[r0:r0 + CONV_CHUNK, :] = (y * jax.nn.sigmoid(y)).astype(out_ref.dtype)


def _mix_ffn_kernel(x_ref, at_ref, u0_ref, un_ref, hn_ref, cw_ref, cb_ref, lg_ref, lb_ref, woc_ref,
                    woa_ref, gpost_ref, gpre_ref, wg_ref, wu_ref, wd_ref, gout_ref, o_ref, win_ref,
                    conv_ref):
    conv_args = (cw_ref, cb_ref, lg_ref, lb_ref, win_ref, conv_ref)
    last_in_seq = pl.program_id(1) == pl.num_programs(1) - 1

    @pl.when((pl.program_id(0) == 0) & (pl.program_id(1) == 0))
    def _():
        _conv_tile(u0_ref[...], jnp.zeros((HALO_ROWS, u0_ref.shape[1]), _f32), *conv_args)

    mix = jnp.dot(conv_ref[...], woc_ref[...], preferred_element_type=_f32)
    mix = mix + lax.dot_general(at_ref[...], woa_ref[...], (((0,), (0,)), ((), ())),
                                preferred_element_type=_f32)
    x1 = x_ref[...] + mix * _rms_scale(mix) * gpost_ref[...]
    h = (x1 * _rms_scale(x1) * gpre_ref[...]).astype(_bf16)
    gate = jnp.dot(h, wg_ref[...], preferred_element_type=_f32)
    up = jnp.dot(h, wu_ref[...], preferred_element_type=_f32)
    act = (gate * jax.nn.sigmoid(gate) * up).astype(_bf16)
    y = jnp.dot(act, wd_ref[...], preferred_element_type=_f32)
    o_ref[...] = x1 + y * _rms_scale(y) * gout_ref[...]

    halo = hn_ref[...].astype(_f32)
    _conv_tile(un_ref[...], jnp.where(last_in_seq, jnp.zeros_like(halo), halo), *conv_args)


def _mix_ffn(x, u, attn_t, conv_w, conv_b, ln_g, ln_b, wo_c, wo_a, g_post, g_pre, w_gate, w_up, w_down,
             g_out, *, tm):
    b, s, d = x.shape
    dc = u.shape[2]
    da = attn_t.shape[1]
    dff = w_gate.shape[1]
    nt = s // tm
    halo_blocks = tm // HALO_ROWS

    def next_tile(i, j):
        flat = jnp.minimum(i * nt + j + 1, b * nt - 1)
        return flat // nt, flat % nt

    def u_next(i, j):
        ni, nj = next_tile(i, j)
        return ni, nj, 0

    def halo_next(i, j):
        ni, nj = next_tile(i, j)
        return ni, jnp.maximum(nj * halo_blocks - 1, 0), 0

    return pl.pallas_call(
        _mix_ffn_kernel,
        grid=(b, nt),
        in_specs=[
            pl.BlockSpec((None, tm, d), lambda i, j: (i, j, 0)),
            pl.BlockSpec((None, da, tm), lambda i, j: (i, 0, j)),
            pl.BlockSpec((None, tm, dc), lambda i, j: (0, 0, 0), pipeline_mode=pl.Buffered(1)),
            pl.BlockSpec((None, tm, dc), u_next),
            pl.BlockSpec((None, HALO_ROWS, dc), halo_next),
            _resident((CONV_WIDTH, dc)),
            _resident((1, dc)),
            _resident((1, dc)),
            _resident((1, dc)),
            _resident((dc, d)),
            _resident((da, d)),
            _resident((1, d)),
            _resident((1, d)),
            _resident((d, dff)),
            _resident((d, dff)),
            _resident((dff, d)),
            _resident((1, d)),
        ],
        out_specs=pl.BlockSpec((None, tm, d), lambda i, j: (i, j, 0)),
        out_shape=jax.ShapeDtypeStruct((b, s, d), x.dtype),
        scratch_shapes=[
            pltpu.VMEM((SUBLANES, tm + HALO_ROWS, dc), _f32),
            pltpu.VMEM((tm, dc), _bf16),
        ],
        compiler_params=_cparams(("arbitrary", "arbitrary")),
        name="mix_ffn",
    )(x, attn_t, u, u, u, conv_w, conv_b, ln_g, ln_b, wo_c, wo_a, g_post, g_pre, w_gate, w_up, w_down, g_out)


def _layer(x, g_mix_pre, w_in, b_forget, conv_w, conv_b, conv_ln_g, conv_ln_b, w_out, g_mix_post,
           g_ffn_pre, w_gate, w_up, w_down, g_ffn_post):
    b, s, d = x.shape
    d_conv = conv_w.shape[1]
    n_heads = b_forget.shape[0]
    d_attn = n_heads * HEAD_DIM
    n_cols = DECAY_SPLITS * n_heads
    assert 2 * d_conv + 3 * d_attn + n_heads == w_in.shape[1]
    assert n_cols <= LANES and d_attn % LANES == 0

    tm = min(512, s)
    t_attn = min(512, s)
    assert s % tm == 0 and s % t_attn == 0 and tm % LANES == 0 and tm % CONV_CHUNK == 0

    o_q, o_k, o_v, o_f = 2 * d_conv, 2 * d_conv + d_attn, 2 * d_conv + 2 * d_attn, 2 * d_conv + 3 * d_attn
    w_f = jnp.repeat(w_in[:, o_f:], DECAY_SPLITS, axis=1)
    w_f = jnp.pad(w_f, ((0, 0), (0, LANES - n_cols)))
    w_nat = jnp.concatenate([w_in[:, :o_q], w_in[:, o_k:o_v], w_f], axis=1).astype(_bf16)
    w_tr = jnp.concatenate([w_in[:, o_q:o_k] * (HEAD_DIM ** -0.5 * LOG2_E), w_in[:, o_v:o_f]], axis=1).T.astype(_bf16)
    bf_ext = jnp.pad(jnp.repeat(b_forget.astype(_f32), DECAY_SPLITS), (0, LANES - n_cols))[None, :]

    row = lambda v: v.astype(_f32)[None, :]
    u, k, aug, qt, vt = _in_proj(x, row(g_mix_pre), w_nat, w_tr, bf_ext, d_conv=d_conv, d_attn=d_attn,
                                 n_cols=n_cols, tm=tm)
    attn_t = _attention(qt, k, aug, vt, t=t_attn, hps=n_heads)
    return _mix_ffn(x, u, attn_t, conv_w.astype(_f32), row(conv_b), row(conv_ln_g), row(conv_ln_b),
                    w_out[:d_conv].astype(_bf16), w_out[d_conv:].astype(_bf16), row(g_mix_post),
                    row(g_ffn_pre), w_gate.astype(_bf16), w_up.astype(_bf16), w_down.astype(_bf16),
                    row(g_ffn_post), tm=tm)


def kernel(x, g_mix_pre, w_in, b_forget, conv_w, conv_b, conv_ln_g, conv_ln_b, w_out, g_mix_post,
           g_ffn_pre, w_gate, w_up, w_down, g_ffn_post):
    for l in range(g_mix_pre.shape[0]):
        x = _layer(x, g_mix_pre[l], w_in[l], b_forget[l], conv_w[l], conv_b[l], conv_ln_g[l],
                   conv_ln_b[l], w_out[l], g_mix_post[l], g_ffn_pre[l], w_gate[l], w_up[l],
                   w_down[l], g_ffn_post[l])
    return x
```

```python
import functools

import jax
import jax.numpy as jnp
from jax import lax
from jax.experimental import pallas as pl
from jax.experimental.pallas import tpu as pltpu

HEAD_DIM = 64
CONV_WIDTH = 31
EPS = 1e-6
LANES = 128
SUBLANES = 8
BF16_ROWS = 16
HALO_ROWS = 32
CONV_CHUNK = 32
DECAY_SPLITS = 3
VMEM_LIMIT_BYTES = 56 * 1024 * 1024
NEG_BIG = -1e30
LOG2_E = 1.4426950408889634

_f32 = jnp.float32
_bf16 = jnp.bfloat16


def _cparams(sem):
    return pltpu.CompilerParams(dimension_semantics=sem, vmem_limit_bytes=VMEM_LIMIT_BYTES)


def _resident(shape):
    nd = len(shape)
    return pl.BlockSpec(shape, lambda *_: (0,) * nd, pipeline_mode=pl.Buffered(1))


def _rms_scale(x):
    return lax.rsqrt(jnp.mean(x * x, axis=-1, keepdims=True) + EPS)


def _split3(x):
    hi = x.astype(_bf16)
    r1 = x - hi.astype(_f32)
    mid = r1.astype(_bf16)
    lo = (r1 - mid.astype(_f32)).astype(_bf16)
    return hi, mid, lo


def _in_proj_kernel(x_ref, g_ref, wf_ref, wn_ref, wt_ref, bf_ref, u_ref, k_ref, aug_ref, qt_ref, vt_ref,
                    carry_ref, *, d_conv, n_cols):
    @pl.when(pl.program_id(1) == 0)
    def _():
        carry_ref[...] = jnp.zeros_like(carry_ref)

    x = x_ref[...]
    tm = x.shape[0]
    d_attn = qt_ref.shape[0]
    h = (x * _rms_scale(x) * g_ref[...]).astype(_bf16)

    z = jnp.dot(h, wf_ref[...], preferred_element_type=_f32) + bf_ref[...]
    nat = jnp.dot(h, wn_ref[...], preferred_element_type=_f32)
    u_ref[...] = (nat[:, :d_conv] * jax.nn.sigmoid(nat[:, d_conv:2 * d_conv])).astype(u_ref.dtype)
    k_ref[...] = nat[:, 2 * d_conv:].astype(k_ref.dtype)

    log_f = jnp.minimum(z, 0.0) - jnp.log1p(jnp.exp(-jnp.abs(z)))
    lf = log_f * LOG2_E
    row = lax.broadcasted_iota(jnp.int32, (LANES, LANES), 0)
    col = lax.broadcasted_iota(jnp.int32, (LANES, LANES), 1)
    tri = (col <= row).astype(_bf16)
    piece = col % DECAY_SPLITS
    local = []
    for gi in range(tm // LANES):
        hi, mid, lo = _split3(lf[gi * LANES:(gi + 1) * LANES, :])
        local.append(jnp.dot(tri, hi, preferred_element_type=_f32)
                     + jnp.dot(tri, mid, preferred_element_type=_f32)
                     + jnp.dot(tri, lo, preferred_element_type=_f32))

    tr = lax.dot_general(wt_ref[...], h, (((1,), (1,)), ((), ())), preferred_element_type=_f32)
    qt_ref[...] = tr[:d_attn].astype(qt_ref.dtype)
    vt_ref[...] = tr[d_attn:].astype(vt_ref.dtype)

    carry = carry_ref[...]
    for gi in range(tm // LANES):
        c = local[gi] + carry
        nhi, nmid, nlo = _split3(-c)
        aug = jnp.where(piece == 0, nhi, jnp.where(piece == 1, nmid, nlo))
        aug_ref[gi * LANES:(gi + 1) * LANES, :] = jnp.where(col < n_cols, aug, jnp.zeros_like(aug))
        carry = c[LANES - 1:LANES, :]
    carry_ref[...] = carry


def _in_proj(x, g, w_f, w_nat, w_tr, bf_ext, *, d_conv, d_attn, n_cols, tm):
    b, s, d = x.shape
    kern = functools.partial(_in_proj_kernel, d_conv=d_conv, n_cols=n_cols)
    return pl.pallas_call(
        kern,
        grid=(b, s // tm),
        in_specs=[
            pl.BlockSpec((None, tm, d), lambda i, j: (i, j, 0)),
            _resident((1, d)),
            _resident((d, LANES)),
            _resident((d, 2 * d_conv + d_attn)),
            _resident((2 * d_attn, d)),
            _resident((1, LANES)),
        ],
        out_specs=[
            pl.BlockSpec((None, tm, d_conv), lambda i, j: (i, j, 0)),
            pl.BlockSpec((None, tm, d_attn), lambda i, j: (i, j, 0)),
            pl.BlockSpec((None, tm, LANES), lambda i, j: (i, j, 0)),
            pl.BlockSpec((None, d_attn, tm), lambda i, j: (i, 0, j)),
            pl.BlockSpec((None, d_attn, tm), lambda i, j: (i, 0, j)),
        ],
        out_shape=[
            jax.ShapeDtypeStruct((b, s, d_conv), _bf16),
            jax.ShapeDtypeStruct((b, s, d_attn), _bf16),
            jax.ShapeDtypeStruct((b, s, LANES), _bf16),
            jax.ShapeDtypeStruct((b, d_attn, s), _bf16),
            jax.ShapeDtypeStruct((b, d_attn, s), _bf16),
        ],
        scratch_shapes=[pltpu.VMEM((1, LANES), _f32)],
        compiler_params=_cparams(("parallel", "arbitrary")),
        name="in_proj",
    )(x, g, w_f, w_nat, w_tr, bf_ext)


def _attn_kernel(qt_ref, k_ref, aug_ref, vt_ref, o_ref, q_ext_ref, s_ref, mx_ref, p_ref, acc_ref, m_ref,
                 alpha_ref, *, t):
    group = pl.program_id(1)
    qi = pl.program_id(2)
    t0 = qi * t
    hd = HEAD_DIM
    acc_rows = acc_ref.shape[1]
    n_heads = acc_ref.shape[0]
    heads = range(n_heads)

    row = lax.broadcasted_iota(jnp.int32, (LANES, t), 0)
    for hh in heads:
        head = n_heads * group + hh
        lo = (hh % 2) * hd
        qt = qt_ref[(hh // 2) * LANES:(hh // 2 + 1) * LANES, :]
        top = jnp.where((row >= lo) & (row < lo + hd), qt, jnp.zeros_like(qt))
        sel = ((row >= DECAY_SPLITS * head) & (row < DECAY_SPLITS * (head + 1))).astype(_bf16)
        q_ext_ref[hh, 0:LANES, :] = top
        q_ext_ref[hh, LANES:, :] = sel
    acc_ref[...] = jnp.zeros_like(acc_ref)
    m_ref[...] = jnp.full_like(m_ref, NEG_BIG)
    alpha_ref[...] = jnp.ones_like(alpha_ref)
    p_ref[1] = jnp.zeros_like(p_ref[1])
    ones_rows = jnp.ones((acc_rows - hd, t), _bf16)

    def logits_into(slot, s0):
        aug = aug_ref[pl.ds(s0, t), :]
        for hh in heads:
            k_pair = k_ref[pl.ds(s0, t), (hh // 2) * LANES:(hh // 2 + 1) * LANES]
            kk = jnp.concatenate([k_pair, aug], axis=1)
            st = jnp.dot(kk, q_ext_ref[hh], preferred_element_type=_f32)
            s_ref[slot, hh] = st
            mx_ref[slot, hh] = jnp.max(st, axis=0, keepdims=True)

    def softmax_stage(slot, hh, masked):
        def tile():
            st = s_ref[slot, hh]
            if masked:
                valid = (lax.broadcasted_iota(jnp.int32, (t, t), 0)
                         <= lax.broadcasted_iota(jnp.int32, (t, t), 1))
                st = jnp.where(valid, st, NEG_BIG)
            return st
        tile_max = jnp.max(tile(), axis=0, keepdims=True) if masked else mx_ref[slot, hh]
        m_old = m_ref[hh]
        m_new = jnp.maximum(m_old, tile_max)
        m_ref[hh] = m_new
        return jnp.exp2(tile() - m_new).astype(_bf16), jnp.exp2(m_old - m_new)

    def weighted_values(hh, s0, p):
        v_ext = jnp.concatenate([vt_ref[hh * hd:(hh + 1) * hd, pl.ds(s0, t)], ones_rows], axis=0)
        return jnp.dot(v_ext, p, preferred_element_type=_f32)

    def trip(j, slot):
        logits_into(1 - slot, pl.multiple_of((j + 1) * t, t))
        s_prev = pl.multiple_of(jnp.maximum(j - 1, 0) * t, t)
        for hh in heads:
            p, alpha = softmax_stage(slot, hh, masked=False)
            acc_ref[hh] = alpha_ref[hh] * acc_ref[hh] + weighted_values(hh, s_prev, p_ref[1 - slot, hh])
            alpha_ref[hh] = alpha
            p_ref[slot, hh] = p

    def finish(slot):
        s_prev = pl.multiple_of(jnp.maximum(qi - 1, 0) * t, t)
        for hh in heads:
            p, alpha = softmax_stage(slot, hh, masked=True)
            acc = alpha_ref[hh] * acc_ref[hh] + weighted_values(hh, s_prev, p_ref[1 - slot, hh])
            acc = alpha * acc + weighted_values(hh, pl.multiple_of(t0, t), p)
            inv_l = 1.0 / acc[hd:hd + 1, :]
            o_ref[hh * hd:(hh + 1) * hd, :] = (acc[:hd, :] * inv_l).astype(o_ref.dtype)

    logits_into(0, 0)

    def two_trips(i, _):
        trip(2 * i, 0)
        trip(2 * i + 1, 1)
        return 0

    lax.fori_loop(0, lax.shift_right_logical(qi, 1), two_trips, 0)
    odd = (qi & 1) == 1

    @pl.when(odd)
    def _():
        trip(qi - 1, 0)
        finish(1)

    @pl.when(jnp.logical_not(odd))
    def _():
        finish(0)


def _attention(qt, k, aug, vt, *, t, hps):
    b, da, s = qt.shape
    w = hps * HEAD_DIM
    acc_rows = HEAD_DIM + BF16_ROWS
    kern = functools.partial(_attn_kernel, t=t)
    return pl.pallas_call(
        kern,
        grid=(b, da // w, s // t),
        in_specs=[
            pl.BlockSpec((None, w, t), lambda i, g, j: (i, g, j)),
            pl.BlockSpec((None, s, w), lambda i, g, j: (i, 0, g), pipeline_mode=pl.Buffered(1)),
            pl.BlockSpec((None, s, LANES), lambda i, g, j: (i, 0, 0), pipeline_mode=pl.Buffered(1)),
            pl.BlockSpec((None, w, s), lambda i, g, j: (i, g, 0), pipeline_mode=pl.Buffered(1)),
        ],
        out_specs=pl.BlockSpec((None, w, t), lambda i, g, j: (i, g, j)),
        out_shape=jax.ShapeDtypeStruct((b, da, s), _bf16),
        scratch_shapes=[
            pltpu.VMEM((hps, 2 * LANES, t), _bf16),
            pltpu.VMEM((2, hps, t, t), _f32),
            pltpu.VMEM((2, hps, 1, t), _f32),
            pltpu.VMEM((2, hps, t, t), _bf16),
            pltpu.VMEM((hps, acc_rows, t), _f32),
            pltpu.VMEM((hps, 1, t), _f32),
            pltpu.VMEM((hps, 1, t), _f32),
        ],
        compiler_params=_cparams(("parallel", "parallel", "arbitrary")),
        name="attention",
    )(qt, k, aug, vt)


def _conv_tile(u_cur, halo, w_ref, cb_ref, lg_ref, lb_ref, win_ref, out_ref):
    tc = u_cur.shape[0]
    win_ref[0, 0:HALO_ROWS, :] = halo
    win_ref[0, HALO_ROWS:, :] = u_cur.astype(_f32)
    n_shift = tc + HALO_ROWS - SUBLANES
    for r in range(1, SUBLANES):
        win_ref[r, 0:n_shift, :] = win_ref[0, r:r + n_shift, :]
    lead = HALO_ROWS - (CONV_WIDTH - 1)
    for r0 in range(0, tc, CONV_CHUNK):
        acc = jnp.broadcast_to(cb_ref[...], (CONV_CHUNK, cb_ref.shape[1]))
        for k in range(CONV_WIDTH):
            phase, base = (lead + k) % SUBLANES, (lead + k) // SUBLANES * SUBLANES
            acc = acc + w_ref[k:k + 1, :] * win_ref[phase, r0 + base:r0 + base + CONV_CHUNK, :]
        mu = jnp.mean(acc, axis=-1, keepdims=True)
        cen = acc - mu
        var = jnp.mean(cen * cen, axis=-1, keepdims=True)
        y = cen * lax.rsqrt(var + EPS) * lg_ref[...] + lb_ref[...]
        out_ref[r0:r0 + CONV_CHUNK, :] = (y * jax.nn.sigmoid(y)).astype(out_ref.dtype)


def _mix_ffn_kernel(x_ref, at_ref, u0_ref, un_ref, hn_ref, cw_ref, cb_ref, lg_ref, lb_ref, woc_ref,
                    woa_ref, gpost_ref, gpre_ref, wg_ref, wu_ref, wd_ref, gout_ref, o_ref, win_ref,
                    conv_ref):
    conv_args = (cw_ref, cb_ref, lg_ref, lb_ref, win_ref, conv_ref)
    last_in_seq = pl.program_id(1) == pl.num_programs(1) - 1

    @pl.when((pl.program_id(0) == 0) & (pl.program_id(1) == 0))
    def _():
        _conv_tile(u0_ref[...], jnp.zeros((HALO_ROWS, u0_ref.shape[1]), _f32), *conv_args)

    tm = x_ref.shape[0]
    halves = [slice(0, tm // 2), slice(tm // 2, tm)]
    mix = [jnp.dot(conv_ref[r, :], woc_ref[...], preferred_element_type=_f32)
           + lax.dot_general(at_ref[:, r], woa_ref[...], (((0,), (0,)), ((), ())),
                             preferred_element_type=_f32) for r in halves]
    x1, gate, up = [], [], []
    for r, mx in zip(halves, mix):
        x1.append(x_ref[r, :] + mx * _rms_scale(mx) * gpost_ref[...])
        h = (x1[-1] * _rms_scale(x1[-1]) * gpre_ref[...]).astype(_bf16)
        gate.append(jnp.dot(h, wg_ref[...], preferred_element_type=_f32))
        up.append(jnp.dot(h, wu_ref[...], preferred_element_type=_f32))
    for r, x1_r, g_r, u_r in zip(halves, x1, gate, up):
        act = (g_r * jax.nn.sigmoid(g_r) * u_r).astype(_bf16)
        y = jnp.dot(act, wd_ref[...], preferred_element_type=_f32)
        o_ref[r, :] = x1_r + y * _rms_scale(y) * gout_ref[...]

    halo = hn_ref[...].astype(_f32)
    _conv_tile(un_ref[...], jnp.where(last_in_seq, jnp.zeros_like(halo), halo), *conv_args)


def _mix_ffn(x, u, attn_t, conv_w, conv_b, ln_g, ln_b, wo_c, wo_a, g_post, g_pre, w_gate, w_up, w_down,
             g_out, *, tm):
    b, s, d = x.shape
    dc = u.shape[2]
    da = attn_t.shape[1]
    dff = w_gate.shape[1]
    nt = s // tm
    halo_blocks = tm // HALO_ROWS

    def next_tile(i, j):
        flat = jnp.minimum(i * nt + j + 1, b * nt - 1)
        return flat // nt, flat % nt

    def u_next(i, j):
        ni, nj = next_tile(i, j)
        return ni, nj, 0

    def halo_next(i, j):
        ni, nj = next_tile(i, j)
        return ni, jnp.maximum(nj * halo_blocks - 1, 0), 0

    return pl.pallas_call(
        _mix_ffn_kernel,
        grid=(b, nt),
        in_specs=[
            pl.BlockSpec((None, tm, d), lambda i, j: (i, j, 0)),
            pl.BlockSpec((None, da, tm), lambda i, j: (i, 0, j)),
            pl.BlockSpec((None, tm, dc), lambda i, j: (0, 0, 0), pipeline_mode=pl.Buffered(1)),
            pl.BlockSpec((None, tm, dc), u_next),
            pl.BlockSpec((None, HALO_ROWS, dc), halo_next),
            _resident((CONV_WIDTH, dc)),
            _resident((1, dc)),
            _resident((1, dc)),
            _resident((1, dc)),
            _resident((dc, d)),
            _resident((da, d)),
            _resident((1, d)),
            _resident((1, d)),
            _resident((d, dff)),
            _resident((d, dff)),
            _resident((dff, d)),
            _resident((1, d)),
        ],
        out_specs=pl.BlockSpec((None, tm, d), lambda i, j: (i, j, 0)),
        out_shape=jax.ShapeDtypeStruct((b, s, d), x.dtype),
        scratch_shapes=[
            pltpu.VMEM((SUBLANES, tm + HALO_ROWS, dc), _f32),
            pltpu.VMEM((tm, dc), _bf16),
        ],
        compiler_params=_cparams(("arbitrary", "arbitrary")),
        name="mix_ffn",
    )(x, attn_t, u, u, u, conv_w, conv_b, ln_g, ln_b, wo_c, wo_a, g_post, g_pre, w_gate, w_up, w_down, g_out)


def _layer(x, g_mix_pre, w_in, b_forget, conv_w, conv_b, conv_ln_g, conv_ln_b, w_out, g_mix_post,
           g_ffn_pre, w_gate, w_up, w_down, g_ffn_post):
    b, s, d = x.shape
    d_conv = conv_w.shape[1]
    n_heads = b_forget.shape[0]
    d_attn = n_heads * HEAD_DIM
    n_cols = DECAY_SPLITS * n_heads
    assert 2 * d_conv + 3 * d_attn + n_heads == w_in.shape[1]
    assert n_cols <= LANES and d_attn % LANES == 0

    tm = min(512, s)
    t_attn = min(512, s)
    assert s % tm == 0 and s % t_attn == 0 and tm % LANES == 0 and tm % CONV_CHUNK == 0

    o_q, o_k, o_v, o_f = 2 * d_conv, 2 * d_conv + d_attn, 2 * d_conv + 2 * d_attn, 2 * d_conv + 3 * d_attn
    w_f = jnp.repeat(w_in[:, o_f:], DECAY_SPLITS, axis=1)
    w_f = jnp.pad(w_f, ((0, 0), (0, LANES - n_cols))).astype(_bf16)
    w_nat = jnp.concatenate([w_in[:, :o_q], w_in[:, o_k:o_v]], axis=1).astype(_bf16)
    w_tr = jnp.concatenate([w_in[:, o_q:o_k] * (HEAD_DIM ** -0.5 * LOG2_E), w_in[:, o_v:o_f]], axis=1).T.astype(_bf16)
    bf_ext = jnp.pad(jnp.repeat(b_forget.astype(_f32), DECAY_SPLITS), (0, LANES - n_cols))[None, :]

    row = lambda v: v.astype(_f32)[None, :]
    u, k, aug, qt, vt = _in_proj(x, row(g_mix_pre), w_f, w_nat, w_tr, bf_ext, d_conv=d_conv, d_attn=d_attn,
                                 n_cols=n_cols, tm=tm)
    attn_t = _attention(qt, k, aug, vt, t=t_attn, hps=n_heads)
    return _mix_ffn(x, u, attn_t, conv_w.astype(_f32), row(conv_b), row(conv_ln_g), row(conv_ln_b),
                    w_out[:d_conv].astype(_bf16), w_out[d_conv:].astype(_bf16), row(g_mix_post),
                    row(g_ffn_pre), w_gate.astype(_bf16), w_up.astype(_bf16), w_down.astype(_bf16),
                    row(g_ffn_post), tm=tm)


def kernel(x, g_mix_pre, w_in, b_forget, conv_w, conv_b, conv_ln_g, conv_ln_b, w_out, g_mix_post,
           g_ffn_pre, w_gate, w_up, w_down, g_ffn_post):
    for l in range(g_mix_pre.shape[0]):
        x = _layer(x, g_mix_pre[l], w_in[l], b_forget[l], conv_w[l], conv_b[l], conv_ln_g[l],
                   conv_ln_b[l], w_out[l], g_mix_post[l], g_ffn_pre[l], w_gate[l], w_up[l],
                   w_down[l], g_ffn_post[l])
    return x
```

```python
import functools

import jax
import jax.numpy as jnp
from jax import lax
from jax.experimental import pallas as pl
from jax.experimental.pallas import tpu as pltpu

HEAD_DIM = 64
CONV_WIDTH = 31
EPS = 1e-6
LANES = 128
SUBLANES = 8
BF16_ROWS = 16
HALO_ROWS = 32
CONV_CHUNK = 32
DECAY_SPLITS = 3
VMEM_LIMIT_BYTES = 56 * 1024 * 1024
NEG_BIG = -1e30
LOG2_E = 1.4426950408889634

_f32 = jnp.float32
_bf16 = jnp.bfloat16


def _cparams(sem):
    return pltpu.CompilerParams(dimension_semantics=sem, vmem_limit_bytes=VMEM_LIMIT_BYTES)


def _resident(shape):
    nd = len(shape)
    return pl.BlockSpec(shape, lambda *_: (0,) * nd, pipeline_mode=pl.Buffered(1))


def _rms_scale(x):
    return lax.rsqrt(jnp.mean(x * x, axis=-1, keepdims=True) + EPS)


def _split3(x):
    hi = x.astype(_bf16)
    r1 = x - hi.astype(_f32)
    mid = r1.astype(_bf16)
    lo = (r1 - mid.astype(_f32)).astype(_bf16)
    return hi, mid, lo


def _in_proj_kernel(x_ref, g_ref, wf_ref, wn_ref, wt_ref, bf_ref, u_ref, k_ref, aug_ref, qt_ref, vt_ref,
                    carry_ref, *, d_conv, n_cols):
    @pl.when(pl.program_id(1) == 0)
    def _():
        carry_ref[...] = jnp.zeros_like(carry_ref)

    x = x_ref[...]
    tm = x.shape[0]
    d_attn = qt_ref.shape[0]
    h = (x * _rms_scale(x) * g_ref[...]).astype(_bf16)

    z = jnp.dot(h, wf_ref[...], preferred_element_type=_f32) + bf_ref[...]
    nat = jnp.dot(h, wn_ref[...], preferred_element_type=_f32)
    u_ref[...] = (nat[:, :d_conv] * jax.nn.sigmoid(nat[:, d_conv:2 * d_conv])).astype(u_ref.dtype)
    k_ref[...] = nat[:, 2 * d_conv:].astype(k_ref.dtype)

    log_f = jnp.minimum(z, 0.0) - jnp.log1p(jnp.exp(-jnp.abs(z)))
    lf = log_f * LOG2_E
    row = lax.broadcasted_iota(jnp.int32, (LANES, LANES), 0)
    col = lax.broadcasted_iota(jnp.int32, (LANES, LANES), 1)
    tri = (col <= row).astype(_bf16)
    piece = col % DECAY_SPLITS
    local = []
    for gi in range(tm // LANES):
        hi, mid, lo = _split3(lf[gi * LANES:(gi + 1) * LANES, :])
        local.append(jnp.dot(tri, hi, preferred_element_type=_f32)
                     + jnp.dot(tri, mid, preferred_element_type=_f32)
                     + jnp.dot(tri, lo, preferred_element_type=_f32))

    tr = lax.dot_general(wt_ref[...], h, (((1,), (1,)), ((), ())), preferred_element_type=_f32)
    qt_ref[...] = tr[:d_attn].astype(qt_ref.dtype)
    vt_ref[...] = tr[d_attn:].astype(vt_ref.dtype)

    carry = carry_ref[...]
    for gi in range(tm // LANES):
        c = local[gi] + carry
        nhi, nmid, nlo = _split3(-c)
        aug = jnp.where(piece == 0, nhi, jnp.where(piece == 1, nmid, nlo))
        aug_ref[gi * LANES:(gi + 1) * LANES, :] = jnp.where(col < n_cols, aug, jnp.zeros_like(aug))
        carry = c[LANES - 1:LANES, :]
    carry_ref[...] = carry


def _in_proj(x, g, w_f, w_nat, w_tr, bf_ext, *, d_conv, d_attn, n_cols, tm):
    b, s, d = x.shape
    kern = functools.partial(_in_proj_kernel, d_conv=d_conv, n_cols=n_cols)
    return pl.pallas_call(
        kern,
        grid=(b, s // tm),
        in_specs=[
            pl.BlockSpec((None, tm, d), lambda i, j: (i, j, 0)),
            _resident((1, d)),
            _resident((d, LANES)),
            _resident((d, 2 * d_conv + d_attn)),
            _resident((2 * d_attn, d)),
            _resident((1, LANES)),
        ],
        out_specs=[
            pl.BlockSpec((None, tm, d_conv), lambda i, j: (i, j, 0)),
            pl.BlockSpec((None, tm, d_attn), lambda i, j: (i, j, 0)),
            pl.BlockSpec((None, tm, LANES), lambda i, j: (i, j, 0)),
            pl.BlockSpec((None, d_attn, tm), lambda i, j: (i, 0, j)),
            pl.BlockSpec((None, d_attn, tm), lambda i, j: (i, 0, j)),
        ],
        out_shape=[
            jax.ShapeDtypeStruct((b, s, d_conv), _bf16),
            jax.ShapeDtypeStruct((b, s, d_attn), _bf16),
            jax.ShapeDtypeStruct((b, s, LANES), _bf16),
            jax.ShapeDtypeStruct((b, d_attn, s), _bf16),
            jax.ShapeDtypeStruct((b, d_attn, s), _bf16),
        ],
        scratch_shapes=[pltpu.VMEM((1, LANES), _f32)],
        compiler_params=_cparams(("parallel", "arbitrary")),
        name="in_proj",
    )(x, g, w_f, w_nat, w_tr, bf_ext)


def _attn_kernel(qt_ref, k_ref, aug_ref, vt_ref, o_ref, q_ext_ref, s_ref, mx_ref, p_ref, acc_ref, m_ref,
                 alpha_ref, *, t):
    group = pl.program_id(1)
    qi = pl.program_id(2)
    t0 = qi * t
    hd = HEAD_DIM
    acc_rows = acc_ref.shape[1]
    n_heads = acc_ref.shape[0]
    heads = range(n_heads)

    row = lax.broadcasted_iota(jnp.int32, (LANES, t), 0)
    for hh in heads:
        head = n_heads * group + hh
        lo = (hh % 2) * hd
        qt = qt_ref[(hh // 2) * LANES:(hh // 2 + 1) * LANES, :]
        top = jnp.where((row >= lo) & (row < lo + hd), qt, jnp.zeros_like(qt))
        sel = ((row >= DECAY_SPLITS * head) & (row < DECAY_SPLITS * (head + 1))).astype(_bf16)
        q_ext_ref[hh, 0:LANES, :] = top
        q_ext_ref[hh, LANES:, :] = sel
    acc_ref[...] = jnp.zeros_like(acc_ref)
    m_ref[...] = jnp.full_like(m_ref, NEG_BIG)
    alpha_ref[...] = jnp.ones_like(alpha_ref)
    p_ref[1] = jnp.zeros_like(p_ref[1])
    ones_rows = jnp.ones((acc_rows - hd, t), _bf16)

    def logits_into(slot, s0):
        aug = aug_ref[pl.ds(s0, t), :]
        for hh in heads:
            k_pair = k_ref[pl.ds(s0, t), (hh // 2) * LANES:(hh // 2 + 1) * LANES]
            kk = jnp.concatenate([k_pair, aug], axis=1)
            st = jnp.dot(kk, q_ext_ref[hh], preferred_element_type=_f32)
            s_ref[slot, hh] = st
            mx_ref[slot, hh] = jnp.max(st, axis=0, keepdims=True)

    def softmax_stage(slot, hh, masked):
        def tile():
            st = s_ref[slot, hh]
            if masked:
                valid = (lax.broadcasted_iota(jnp.int32, (t, t), 0)
                         <= lax.broadcasted_iota(jnp.int32, (t, t), 1))
                st = jnp.where(valid, st, NEG_BIG)
            return st
        tile_max = jnp.max(tile(), axis=0, keepdims=True) if masked else mx_ref[slot, hh]
        m_old = m_ref[hh]
        m_new = jnp.maximum(m_old, tile_max)
        m_ref[hh] = m_new
        return jnp.exp2(tile() - m_new).astype(_bf16), jnp.exp2(m_old - m_new)

    def weighted_values(hh, s0, p):
        v_ext = jnp.concatenate([vt_ref[hh * hd:(hh + 1) * hd, pl.ds(s0, t)], ones_rows], axis=0)
        return jnp.dot(v_ext, p, preferred_element_type=_f32)

    def trip(j, slot):
        logits_into(1 - slot, pl.multiple_of((j + 1) * t, t))
        s_prev = pl.multiple_of(jnp.maximum(j - 1, 0) * t, t)
        for hh in heads:
            p, alpha = softmax_stage(slot, hh, masked=False)
            acc_ref[hh] = alpha_ref[hh] * acc_ref[hh] + weighted_values(hh, s_prev, p_ref[1 - slot, hh])
            alpha_ref[hh] = alpha
            p_ref[slot, hh] = p

    def finish(slot):
        s_prev = pl.multiple_of(jnp.maximum(qi - 1, 0) * t, t)
        for hh in heads:
            p, alpha = softmax_stage(slot, hh, masked=True)
            acc_ref[hh] = alpha_ref[hh] * acc_ref[hh] + weighted_values(hh, s_prev, p_ref[1 - slot, hh])
            alpha_ref[hh] = alpha
            p_ref[slot, hh] = p
        for hh in heads:
            acc = alpha_ref[hh] * acc_ref[hh] + weighted_values(hh, pl.multiple_of(t0, t), p_ref[slot, hh])
            inv_l = 1.0 / acc[hd:hd + 1, :]
            o_ref[hh * hd:(hh + 1) * hd, :] = (acc[:hd, :] * inv_l).astype(o_ref.dtype)

    logits_into(0, 0)

    def two_trips(i, _):
        trip(2 * i, 0)
        trip(2 * i + 1, 1)
        return 0

    lax.fori_loop(0, lax.shift_right_logical(qi, 1), two_trips, 0)
    odd = (qi & 1) == 1

    @pl.when(odd)
    def _():
        trip(qi - 1, 0)
        finish(1)

    @pl.when(jnp.logical_not(odd))
    def _():
        finish(0)


def _attention(qt, k, aug, vt, *, t, hps):
    b, da, s = qt.shape
    w = hps * HEAD_DIM
    acc_rows = HEAD_DIM + BF16_ROWS
    kern = functools.partial(_attn_kernel, t=t)
    return pl.pallas_call(
        kern,
        grid=(b, da // w, s // t),
        in_specs=[
            pl.BlockSpec((None, w, t), lambda i, g, j: (i, g, j)),
            pl.BlockSpec((None, s, w), lambda i, g, j: (i, 0, g), pipeline_mode=pl.Buffered(1)),
            pl.BlockSpec((None, s, LANES), lambda i, g, j: (i, 0, 0), pipeline_mode=pl.Buffered(1)),
            pl.BlockSpec((None, w, s), lambda i, g, j: (i, g, 0), pipeline_mode=pl.Buffered(1)),
        ],
        out_specs=pl.BlockSpec((None, w, t), lambda i, g, j: (i, g, j)),
        out_shape=jax.ShapeDtypeStruct((b, da, s), _bf16),
        scratch_shapes=[
            pltpu.VMEM((hps, 2 * LANES, t), _bf16),
            pltpu.VMEM((2, hps, t, t), _f32),
            pltpu.VMEM((2, hps, 1, t), _f32),
            pltpu.VMEM((2, hps, t, t), _bf16),
            pltpu.VMEM((hps, acc_rows, t), _f32),
            pltpu.VMEM((hps, 1, t), _f32),
            pltpu.VMEM((hps, 1, t), _f32),
        ],
        compiler_params=_cparams(("parallel", "parallel", "arbitrary")),
        name="attention",
    )(qt, k, aug, vt)


def _conv_tile(u_cur, halo, w_ref, cb_ref, lg_ref, lb_ref, win_ref, out_ref):
    tc = u_cur.shape[0]
    win_ref[0, 0:HALO_ROWS, :] = halo
    win_ref[0, HALO_ROWS:, :] = u_cur.astype(_f32)
    n_shift = tc + HALO_ROWS - SUBLANES
    for r in range(1, SUBLANES):
        win_ref[r, 0:n_shift, :] = win_ref[0, r:r + n_shift, :]
    lead = HALO_ROWS - (CONV_WIDTH - 1)
    for r0 in range(0, tc, CONV_CHUNK):
        acc = jnp.broadcast_to(cb_ref[...], (CONV_CHUNK, cb_ref.shape[1]))
        for k in range(CONV_WIDTH):
            phase, base = (lead + k) % SUBLANES, (lead + k) // SUBLANES * SUBLANES
            acc = acc + w_ref[k:k + 1, :] * win_ref[phase, r0 + base:r0 + base + CONV_CHUNK, :]
        mu = jnp.mean(acc, axis=-1, keepdims=True)
        cen = acc - mu
        var = jnp.mean(cen * cen, axis=-1, keepdims=True)
        y = cen * lax.rsqrt(var + EPS) * lg_ref[...] + lb_ref[...]
        out_ref[r0:r0 + CONV_CHUNK, :] = (y * jax.nn.sigmoid(y)).astype(out_ref.dtype)


def _mix_ffn_kernel(x_ref, at_ref, u0_ref, un_ref, hn_ref, cw_ref, cb_ref, lg_ref, lb_ref, woc_ref,
                    woa_ref, gpost_ref, gpre_ref, wg_ref, wu_ref, wd_ref, gout_ref, o_ref, win_ref,
                    conv_ref):
    conv_args = (cw_ref, cb_ref, lg_ref, lb_ref, win_ref, conv_ref)
    last_in_seq = pl.program_id(1) == pl.num_programs(1) - 1

    @pl.when((pl.program_id(0) == 0) & (pl.program_id(1) == 0))
    def _():
        _conv_tile(u0_ref[...], jnp.zeros((HALO_ROWS, u0_ref.shape[1]), _f32), *conv_args)

    tm = x_ref.shape[0]
    halves = [slice(0, tm // 2), slice(tm // 2, tm)]
    mix = [jnp.dot(conv_ref[r, :], woc_ref[...], preferred_element_type=_f32)
           + lax.dot_general(at_ref[:, r], woa_ref[...], (((0,), (0,)), ((), ())),
                             preferred_element_type=_f32) for r in halves]
    x1, gate, up = [], [], []
    for r, mx in zip(halves, mix):
        x1.append(x_ref[r, :] + mx * _rms_scale(mx) * gpost_ref[...])
        h = (x1[-1] * _rms_scale(x1[-1]) * gpre_ref[...]).astype(_bf16)
        gate.append(jnp.dot(h, wg_ref[...], preferred_element_type=_f32))
        up.append(jnp.dot(h, wu_ref[...], preferred_element_type=_f32))
    for r, x1_r, g_r, u_r in zip(halves, x1, gate, up):
        act = (g_r * jax.nn.sigmoid(g_r) * u_r).astype(_bf16)
        y = jnp.dot(act, wd_ref[...], preferred_element_type=_f32)
        o_ref[r, :] = x1_r + y * _rms_scale(y) * gout_ref[...]

    halo = hn_ref[...].astype(_f32)
    _conv_tile(un_ref[...], jnp.where(last_in_seq, jnp.zeros_like(halo), halo), *conv_args)


def _mix_ffn(x, u, attn_t, conv_w, conv_b, ln_g, ln_b, wo_c, wo_a, g_post, g_pre, w_gate, w_up, w_down,
             g_out, *, tm):
    b, s, d = x.shape
    dc = u.shape[2]
    da = attn_t.shape[1]
    dff = w_gate.shape[1]
    nt = s // tm
    halo_blocks = tm // HALO_ROWS

    def next_tile(i, j):
        flat = jnp.minimum(i * nt + j + 1, b * nt - 1)
        return flat // nt, flat % nt

    def u_next(i, j):
        ni, nj = next_tile(i, j)
        return ni, nj, 0

    def halo_next(i, j):
        ni, nj = next_tile(i, j)
        return ni, jnp.maximum(nj * halo_blocks - 1, 0), 0

    return pl.pallas_call(
        _mix_ffn_kernel,
        grid=(b, nt),
        in_specs=[
            pl.BlockSpec((None, tm, d), lambda i, j: (i, j, 0)),
            pl.BlockSpec((None, da, tm), lambda i, j: (i, 0, j)),
            pl.BlockSpec((None, tm, dc), lambda i, j: (0, 0, 0), pipeline_mode=pl.Buffered(1)),
            pl.BlockSpec((None, tm, dc), u_next),
            pl.BlockSpec((None, HALO_ROWS, dc), halo_next),
            _resident((CONV_WIDTH, dc)),
            _resident((1, dc)),
            _resident((1, dc)),
            _resident((1, dc)),
            _resident((dc, d)),
            _resident((da, d)),
            _resident((1, d)),
            _resident((1, d)),
            _resident((d, dff)),
            _resident((d, dff)),
            _resident((dff, d)),
            _resident((1, d)),
        ],
        out_specs=pl.BlockSpec((None, tm, d), lambda i, j: (i, j, 0)),
        out_shape=jax.ShapeDtypeStruct((b, s, d), x.dtype),
        scratch_shapes=[
            pltpu.VMEM((SUBLANES, tm + HALO_ROWS, dc), _f32),
            pltpu.VMEM((tm, dc), _bf16),
        ],
        compiler_params=_cparams(("arbitrary", "arbitrary")),
        name="mix_ffn",
    )(x, attn_t, u, u, u, conv_w, conv_b, ln_g, ln_b, wo_c, wo_a, g_post, g_pre, w_gate, w_up, w_down, g_out)


def _layer(x, g_mix_pre, w_in, b_forget, conv_w, conv_b, conv_ln_g, conv_ln_b, w_out, g_mix_post,
           g_ffn_pre, w_gate, w_up, w_down, g_ffn_post):
    b, s, d = x.shape
    d_conv = conv_w.shape[1]
    n_heads = b_forget.shape[0]
    d_attn = n_heads * HEAD_DIM
    n_cols = DECAY_SPLITS * n_heads
    assert 2 * d_conv + 3 * d_attn + n_heads == w_in.shape[1]
    assert n_cols <= LANES and d_attn % LANES == 0

    tm = min(512, s)
    t_attn = min(512, s)
    assert s % tm == 0 and s % t_attn == 0 and tm % LANES == 0 and tm % CONV_CHUNK == 0

    o_q, o_k, o_v, o_f = 2 * d_conv, 2 * d_conv + d_attn, 2 * d_conv + 2 * d_attn, 2 * d_conv + 3 * d_attn
    w_f = jnp.repeat(w_in[:, o_f:], DECAY_SPLITS, axis=1)
    w_f = jnp.pad(w_f, ((0, 0), (0, LANES - n_cols))).astype(_bf16)
    w_nat = jnp.concatenate([w_in[:, :o_q], w_in[:, o_k:o_v]], axis=1).astype(_bf16)
    w_tr = jnp.concatenate([w_in[:, o_q:o_k] * (HEAD_DIM ** -0.5 * LOG2_E), w_in[:, o_v:o_f]], axis=1).T.astype(_bf16)
    bf_ext = jnp.pad(jnp.repeat(b_forget.astype(_f32), DECAY_SPLITS), (0, LANES - n_cols))[None, :]

    row = lambda v: v.astype(_f32)[None, :]
    u, k, aug, qt, vt = _in_proj(x, row(g_mix_pre), w_f, w_nat, w_tr, bf_ext, d_conv=d_conv, d_attn=d_attn,
                                 n_cols=n_cols, tm=tm)
    attn_t = _attention(qt, k, aug, vt, t=t_attn, hps=n_heads)
    return _mix_ffn(x, u, attn_t, conv_w.astype(_f32), row(conv_b), row(conv_ln_g), row(conv_ln_b),
                    w_out[:d_conv].astype(_bf16), w_out[d_conv:].astype(_bf16), row(g_mix_post),
                    row(g_ffn_pre), w_gate.astype(_bf16), w_up.astype(_bf16), w_down.astype(_bf16),
                    row(g_ffn_post), tm=tm)


def kernel(x, g_mix_pre, w_in, b_forget, conv_w, conv_b, conv_ln_g, conv_ln_b, w_out, g_mix_post,
           g_ffn_pre, w_gate, w_up, w_down, g_ffn_post):
    for l in range(g_mix_pre.shape[0]):
        x = _layer(x, g_mix_pre[l], w_in[l], b_forget[l], conv_w[l], conv_b[l], conv_ln_g[l],
                   conv_ln_b[l], w_out[l], g_mix_post[l], g_ffn_pre[l], w_gate[l], w_up[l],
                   w_down[l], g_ffn_post[l])
    return x
```

```python
import functools
from typing import NamedTuple

import jax
import jax.numpy as jnp
from jax import lax
from jax.experimental import pallas as pl
from jax.experimental.pallas import tpu as pltpu

HEAD_DIM = 64
CONV_WIDTH = 31
EPS = 1e-6
LANES = 128
SUBLANES = 8
BF16_ROWS = 16
HALO_ROWS = 32
CONV_CHUNK = 32
DECAY_SPLITS = 3
VMEM_LIMIT_BYTES = 56 * 1024 * 1024
NEG_BIG = -1e30
LOG2_E = 1.4426950408889634

_f32 = jnp.float32
_bf16 = jnp.bfloat16


def _cparams(sem):
    return pltpu.CompilerParams(dimension_semantics=sem, vmem_limit_bytes=VMEM_LIMIT_BYTES)


def _resident(shape):
    nd = len(shape)
    return pl.BlockSpec(shape, lambda *_: (0,) * nd, pipeline_mode=pl.Buffered(1))


def _rms_scale(x):
    return lax.rsqrt(jnp.mean(x * x, axis=-1, keepdims=True) + EPS)


def _split3(x):
    hi = x.astype(_bf16)
    r1 = x - hi.astype(_f32)
    mid = r1.astype(_bf16)
    lo = (r1 - mid.astype(_f32)).astype(_bf16)
    return hi, mid, lo


def _in_proj_kernel(x_ref, g_ref, wf_ref, wn_ref, wt_ref, bf_ref, u_ref, k_ref, aug_ref, qt_ref, vt_ref,
                    carry_ref, *, d_conv, n_cols):
    @pl.when(pl.program_id(1) == 0)
    def _():
        carry_ref[...] = jnp.zeros_like(carry_ref)

    x = x_ref[...]
    tm = x.shape[0]
    d_attn = qt_ref.shape[0]
    h = (x * _rms_scale(x) * g_ref[...]).astype(_bf16)

    z = jnp.dot(h, wf_ref[...], preferred_element_type=_f32) + bf_ref[...]
    nat = jnp.dot(h, wn_ref[...], preferred_element_type=_f32)
    u_ref[...] = (nat[:, :d_conv] * jax.nn.sigmoid(nat[:, d_conv:2 * d_conv])).astype(u_ref.dtype)
    k_ref[...] = nat[:, 2 * d_conv:].astype(k_ref.dtype)

    log_f = jnp.minimum(z, 0.0) - jnp.log1p(jnp.exp(-jnp.abs(z)))
    lf = log_f * LOG2_E
    row = lax.broadcasted_iota(jnp.int32, (LANES, LANES), 0)
    col = lax.broadcasted_iota(jnp.int32, (LANES, LANES), 1)
    tri = (col <= row).astype(_bf16)
    piece = col % DECAY_SPLITS
    local = []
    for gi in range(tm // LANES):
        hi, mid, lo = _split3(lf[gi * LANES:(gi + 1) * LANES, :])
        local.append(jnp.dot(tri, hi, preferred_element_type=_f32)
                     + jnp.dot(tri, mid, preferred_element_type=_f32)
                     + jnp.dot(tri, lo, preferred_element_type=_f32))

    tr = lax.dot_general(wt_ref[...], h, (((1,), (1,)), ((), ())), preferred_element_type=_f32)
    qt_ref[...] = tr[:d_attn].astype(qt_ref.dtype)
    vt_ref[...] = tr[d_attn:].astype(vt_ref.dtype)

    carry = carry_ref[...]
    for gi in range(tm // LANES):
        c = local[gi] + carry
        nhi, nmid, nlo = _split3(-c)
        aug = jnp.where(piece == 0, nhi, jnp.where(piece == 1, nmid, nlo))
        aug_ref[gi * LANES:(gi + 1) * LANES, :] = jnp.where(col < n_cols, aug, jnp.zeros_like(aug))
        carry = c[LANES - 1:LANES, :]
    carry_ref[...] = carry


def _in_proj(x, g, w_f, w_nat, w_tr, bf_ext, *, d_conv, d_attn, n_cols, tm):
    b, s, d = x.shape
    kern = functools.partial(_in_proj_kernel, d_conv=d_conv, n_cols=n_cols)
    return pl.pallas_call(
        kern,
        grid=(b, s // tm),
        in_specs=[
            pl.BlockSpec((None, tm, d), lambda i, j: (i, j, 0)),
            _resident((1, d)),
            _resident((d, LANES)),
            _resident((d, 2 * d_conv + d_attn)),
            _resident((2 * d_attn, d)),
            _resident((1, LANES)),
        ],
        out_specs=[
            pl.BlockSpec((None, tm, d_conv), lambda i, j: (i, j, 0)),
            pl.BlockSpec((None, tm, d_attn), lambda i, j: (i, j, 0)),
            pl.BlockSpec((None, tm, LANES), lambda i, j: (i, j, 0)),
            pl.BlockSpec((None, d_attn, tm), lambda i, j: (i, 0, j)),
            pl.BlockSpec((None, d_attn, tm), lambda i, j: (i, 0, j)),
        ],
        out_shape=[
            jax.ShapeDtypeStruct((b, s, d_conv), _bf16),
            jax.ShapeDtypeStruct((b, s, d_attn), _bf16),
            jax.ShapeDtypeStruct((b, s, LANES), _bf16),
            jax.ShapeDtypeStruct((b, d_attn, s), _bf16),
            jax.ShapeDtypeStruct((b, d_attn, s), _bf16),
        ],
        scratch_shapes=[pltpu.VMEM((1, LANES), _f32)],
        compiler_params=_cparams(("parallel", "arbitrary")),
        name="in_proj",
    )(x, g, w_f, w_nat, w_tr, bf_ext)


def _attn_kernel(qt_ref, k_ref, aug_ref, vt_ref, o_ref, q_ext_ref, s_ref, mx_ref, p_ref, acc_ref, m_ref,
                 alpha_ref, *, t):
    group = pl.program_id(1)
    qi = pl.program_id(2)
    t0 = qi * t
    hd = HEAD_DIM
    acc_rows = acc_ref.shape[1]
    n_heads = acc_ref.shape[0]
    heads = range(n_heads)

    row = lax.broadcasted_iota(jnp.int32, (LANES, t), 0)
    for hh in heads:
        head = n_heads * group + hh
        lo = (hh % 2) * hd
        qt = qt_ref[(hh // 2) * LANES:(hh // 2 + 1) * LANES, :]
        top = jnp.where((row >= lo) & (row < lo + hd), qt, jnp.zeros_like(qt))
        sel = ((row >= DECAY_SPLITS * head) & (row < DECAY_SPLITS * (head + 1))).astype(_bf16)
        q_ext_ref[hh, 0:LANES, :] = top
        q_ext_ref[hh, LANES:, :] = sel
    acc_ref[...] = jnp.zeros_like(acc_ref)
    m_ref[...] = jnp.full_like(m_ref, NEG_BIG)
    alpha_ref[...] = jnp.ones_like(alpha_ref)
    p_ref[1] = jnp.zeros_like(p_ref[1])
    ones_rows = jnp.ones((acc_rows - hd, t), _bf16)

    def logits_into(slot, s0):
        aug = aug_ref[pl.ds(s0, t), :]
        for hh in heads:
            k_pair = k_ref[pl.ds(s0, t), (hh // 2) * LANES:(hh // 2 + 1) * LANES]
            kk = jnp.concatenate([k_pair, aug], axis=1)
            st = jnp.dot(kk, q_ext_ref[hh], preferred_element_type=_f32)
            s_ref[slot, hh] = st
            mx_ref[slot, hh] = jnp.max(st, axis=0, keepdims=True)

    def softmax_stage(slot, hh, masked):
        def tile():
            st = s_ref[slot, hh]
            if masked:
                valid = (lax.broadcasted_iota(jnp.int32, (t, t), 0)
                         <= lax.broadcasted_iota(jnp.int32, (t, t), 1))
                st = jnp.where(valid, st, NEG_BIG)
            return st
        tile_max = jnp.max(tile(), axis=0, keepdims=True) if masked else mx_ref[slot, hh]
        m_old = m_ref[hh]
        m_new = jnp.maximum(m_old, tile_max)
        m_ref[hh] = m_new
        return jnp.exp2(tile() - m_new).astype(_bf16), jnp.exp2(m_old - m_new)

    def weighted_values(hh, s0, p):
        v_ext = jnp.concatenate([vt_ref[hh * hd:(hh + 1) * hd, pl.ds(s0, t)], ones_rows], axis=0)
        return jnp.dot(v_ext, p, preferred_element_type=_f32)

    def trip(j, slot):
        logits_into(1 - slot, pl.multiple_of((j + 1) * t, t))
        s_prev = pl.multiple_of(jnp.maximum(j - 1, 0) * t, t)
        for hh in heads:
            p, alpha = softmax_stage(slot, hh, masked=False)
            acc_ref[hh] = alpha_ref[hh] * acc_ref[hh] + weighted_values(hh, s_prev, p_ref[1 - slot, hh])
            alpha_ref[hh] = alpha
            p_ref[slot, hh] = p

    def finish(slot):
        s_prev = pl.multiple_of(jnp.maximum(qi - 1, 0) * t, t)
        for hh in heads:
            p, alpha = softmax_stage(slot, hh, masked=True)
            acc = alpha_ref[hh] * acc_ref[hh] + weighted_values(hh, s_prev, p_ref[1 - slot, hh])
            acc = alpha * acc + weighted_values(hh, pl.multiple_of(t0, t), p)
            inv_l = 1.0 / acc[hd:hd + 1, :]
            o_ref[hh * hd:(hh + 1) * hd, :] = (acc[:hd, :] * inv_l).astype(o_ref.dtype)

    logits_into(0, 0)

    def two_trips(i, _):
        trip(2 * i, 0)
        trip(2 * i + 1, 1)
        return 0

    lax.fori_loop(0, lax.shift_right_logical(qi, 1), two_trips, 0)
    odd = (qi & 1) == 1

    @pl.when(odd)
    def _():
        trip(qi - 1, 0)
        finish(1)

    @pl.when(jnp.logical_not(odd))
    def _():
        finish(0)


def _attention(qt, k, aug, vt, *, t, hps):
    b, da, s = qt.shape
    w = hps * HEAD_DIM
    acc_rows = HEAD_DIM + BF16_ROWS
    kern = functools.partial(_attn_kernel, t=t)
    return pl.pallas_call(
        kern,
        grid=(b, da // w, s // t),
        in_specs=[
            pl.BlockSpec((None, w, t), lambda i, g, j: (i, g, j)),
            pl.BlockSpec((None, s, w), lambda i, g, j: (i, 0, g), pipeline_mode=pl.Buffered(1)),
            pl.BlockSpec((None, s, LANES), lambda i, g, j: (i, 0, 0), pipeline_mode=pl.Buffered(1)),
            pl.BlockSpec((None, w, s), lambda i, g, j: (i, g, 0), pipeline_mode=pl.Buffered(1)),
        ],
        out_specs=pl.BlockSpec((None, w, t), lambda i, g, j: (i, g, j)),
        out_shape=jax.ShapeDtypeStruct((b, da, s), _bf16),
        scratch_shapes=[
            pltpu.VMEM((hps, 2 * LANES, t), _bf16),
            pltpu.VMEM((2, hps, t, t), _f32),
            pltpu.VMEM((2, hps, 1, t), _f32),
            pltpu.VMEM((2, hps, t, t), _bf16),
            pltpu.VMEM((hps, acc_rows, t), _f32),
            pltpu.VMEM((hps, 1, t), _f32),
            pltpu.VMEM((hps, 1, t), _f32),
        ],
        compiler_params=_cparams(("parallel", "parallel", "arbitrary")),
        name="attention",
    )(qt, k, aug, vt)


def _conv_tile(u_cur, halo, w_ref, cb_ref, lg_ref, lb_ref, win_ref, out_ref):
    tc = u_cur.shape[0]
    win_ref[0, 0:HALO_ROWS, :] = halo
    win_ref[0, HALO_ROWS:, :] = u_cur.astype(_f32)
    n_shift = tc + HALO_ROWS - SUBLANES
    for r in range(1, SUBLANES):
        win_ref[r, 0:n_shift, :] = win_ref[0, r:r + n_shift, :]
    lead = HALO_ROWS - (CONV_WIDTH - 1)
    groups = (CONV_CHUNK // SUBLANES, SUBLANES, cb_ref.shape[1])
    for r0 in range(0, tc, CONV_CHUNK):
        acc = jnp.broadcast_to(cb_ref[...], groups)
        for k in range(CONV_WIDTH):
            phase, base = (lead + k) % SUBLANES, (lead + k) // SUBLANES * SUBLANES
            tap = win_ref[phase, r0 + base:r0 + base + CONV_CHUNK, :].reshape(groups)
            acc = acc + w_ref[k] * tap
        acc = acc.reshape(CONV_CHUNK, groups[2])
        mu = jnp.mean(acc, axis=-1, keepdims=True)
        cen = acc - mu
        var = jnp.mean(cen * cen, axis=-1, keepdims=True)
        y = cen * lax.rsqrt(var + EPS) * lg_ref[...] + lb_ref[...]
        out_ref[r0:r0 + CONV_CHUNK, :] = (y * jax.nn.sigmoid(y)).astype(out_ref.dtype)


def _mix_ffn_kernel(x_ref, at_ref, u0_ref, un_ref, hn_ref, cw_ref, cb_ref, lg_ref, lb_ref, woc_ref,
                    woa_ref, gpost_ref, gpre_ref, wg_ref, wu_ref, wd_ref, gout_ref, o_ref, win_ref,
                    conv_ref):
    conv_args = (cw_ref, cb_ref, lg_ref, lb_ref, win_ref, conv_ref)
    last_in_seq = pl.program_id(1) == pl.num_programs(1) - 1

    @pl.when((pl.program_id(0) == 0) & (pl.program_id(1) == 0))
    def _():
        _conv_tile(u0_ref[...], jnp.zeros((HALO_ROWS, u0_ref.shape[1]), _f32), *conv_args)

    tm = x_ref.shape[0]
    halves = [slice(0, tm // 2), slice(tm // 2, tm)]
    mix = [jnp.dot(conv_ref[r, :], woc_ref[...], preferred_element_type=_f32)
           + lax.dot_general(at_ref[:, r], woa_ref[...], (((0,), (0,)), ((), ())),
                             preferred_element_type=_f32) for r in halves]
    x1, gate, up = [], [], []
    for r, mx in zip(halves, mix):
        x1.append(x_ref[r, :] + mx * _rms_scale(mx) * gpost_ref[...])
        h = (x1[-1] * _rms_scale(x1[-1]) * gpre_ref[...]).astype(_bf16)
        gate.append(jnp.dot(h, wg_ref[...], preferred_element_type=_f32))
        up.append(jnp.dot(h, wu_ref[...], preferred_element_type=_f32))
    for r, x1_r, g_r, u_r in zip(halves, x1, gate, up):
        act = (g_r * jax.nn.sigmoid(g_r) * u_r).astype(_bf16)
        y = jnp.dot(act, wd_ref[...], preferred_element_type=_f32)
        o_ref[r, :] = x1_r + y * _rms_scale(y) * gout_ref[...]

    halo = hn_ref[...].astype(_f32)
    _conv_tile(un_ref[...], jnp.where(last_in_seq, jnp.zeros_like(halo), halo), *conv_args)


def _mix_ffn(x, u, attn_t, conv_w, conv_b, ln_g, ln_b, wo_c, wo_a, g_post, g_pre, w_gate, w_up, w_down,
             g_out, *, tm):
    b, s, d = x.shape
    dc = u.shape[2]
    da = attn_t.shape[1]
    dff = w_gate.shape[1]
    nt = s // tm
    halo_blocks = tm // HALO_ROWS

    def next_tile(i, j):
        flat = jnp.minimum(i * nt + j + 1, b * nt - 1)
        return flat // nt, flat % nt

    def u_next(i, j):
        ni, nj = next_tile(i, j)
        return ni, nj, 0

    def halo_next(i, j):
        ni, nj = next_tile(i, j)
        return ni, jnp.maximum(nj * halo_blocks - 1, 0), 0

    return pl.pallas_call(
        _mix_ffn_kernel,
        grid=(b, nt),
        in_specs=[
            pl.BlockSpec((None, tm, d), lambda i, j: (i, j, 0)),
            pl.BlockSpec((None, da, tm), lambda i, j: (i, 0, j)),
            pl.BlockSpec((None, tm, dc), lambda i, j: (0, 0, 0), pipeline_mode=pl.Buffered(1)),
            pl.BlockSpec((None, tm, dc), u_next),
            pl.BlockSpec((None, HALO_ROWS, dc), halo_next),
            _resident((CONV_WIDTH, SUBLANES, dc)),
            _resident((SUBLANES, dc)),
            _resident((1, dc)),
            _resident((1, dc)),
            _resident((dc, d)),
            _resident((da, d)),
            _resident((1, d)),
            _resident((1, d)),
            _resident((d, dff)),
            _resident((d, dff)),
            _resident((dff, d)),
            _resident((1, d)),
        ],
        out_specs=pl.BlockSpec((None, tm, d), lambda i, j: (i, j, 0)),
        out_shape=jax.ShapeDtypeStruct((b, s, d), x.dtype),
        scratch_shapes=[
            pltpu.VMEM((SUBLANES, tm + HALO_ROWS, dc), _f32),
            pltpu.VMEM((tm, dc), _bf16),
        ],
        compiler_params=_cparams(("arbitrary", "arbitrary")),
        name="mix_ffn",
    )(x, attn_t, u, u, u, conv_w, conv_b, ln_g, ln_b, wo_c, wo_a, g_post, g_pre, w_gate, w_up, w_down, g_out)


class _Tiles(NamedTuple):
    proj_rows: int
    attn: int
    ffn_rows: int


def _tile_sizes(s):
    tiles = _Tiles(proj_rows=min(1024, s), attn=min(512, s), ffn_rows=min(512, s))
    assert all(s % n == 0 and n % LANES == 0 for n in tiles) and tiles.ffn_rows % CONV_CHUNK == 0
    return tiles


def _layer(x, g_mix_pre, w_in, b_forget, conv_w, conv_b, conv_ln_g, conv_ln_b, w_out, g_mix_post,
           g_ffn_pre, w_gate, w_up, w_down, g_ffn_post):
    b, s, d = x.shape
    d_conv = conv_w.shape[1]
    n_heads = b_forget.shape[0]
    d_attn = n_heads * HEAD_DIM
    n_cols = DECAY_SPLITS * n_heads
    assert 2 * d_conv + 3 * d_attn + n_heads == w_in.shape[1]
    assert n_cols <= LANES and d_attn % LANES == 0

    tiles = _tile_sizes(s)

    o_q, o_k, o_v, o_f = 2 * d_conv, 2 * d_conv + d_attn, 2 * d_conv + 2 * d_attn, 2 * d_conv + 3 * d_attn
    w_f = jnp.repeat(w_in[:, o_f:], DECAY_SPLITS, axis=1)
    w_f = jnp.pad(w_f, ((0, 0), (0, LANES - n_cols))).astype(_bf16)
    w_nat = jnp.concatenate([w_in[:, :o_q], w_in[:, o_k:o_v]], axis=1).astype(_bf16)
    w_tr = jnp.concatenate([w_in[:, o_q:o_k] * (HEAD_DIM ** -0.5 * LOG2_E), w_in[:, o_v:o_f]], axis=1).T.astype(_bf16)
    bf_ext = jnp.pad(jnp.repeat(b_forget.astype(_f32), DECAY_SPLITS), (0, LANES - n_cols))[None, :]

    row = lambda v: v.astype(_f32)[None, :]
    u, k, aug, qt, vt = _in_proj(x, row(g_mix_pre), w_f, w_nat, w_tr, bf_ext, d_conv=d_conv, d_attn=d_attn,
                                 n_cols=n_cols, tm=tiles.proj_rows)
    attn_t = _attention(qt, k, aug, vt, t=tiles.attn, hps=n_heads)
    over_sublanes = lambda v: jnp.broadcast_to(v.astype(_f32)[..., None, :], v.shape[:-1] + (SUBLANES, v.shape[-1]))
    return _mix_ffn(x, u, attn_t, over_sublanes(conv_w), over_sublanes(conv_b), row(conv_ln_g), row(conv_ln_b),
                    w_out[:d_conv].astype(_bf16), w_out[d_conv:].astype(_bf16), row(g_mix_post),
                    row(g_ffn_pre), w_gate.astype(_bf16), w_up.astype(_bf16), w_down.astype(_bf16),
                    row(g_ffn_post), tm=tiles.ffn_rows)


def kernel(x, g_mix_pre, w_in, b_forget, conv_w, conv_b, conv_ln_g, conv_ln_b, w_out, g_mix_post,
           g_ffn_pre, w_gate, w_up, w_down, g_ffn_post):
    for l in range(g_mix_pre.shape[0]):
        x = _layer(x, g_mix_pre[l], w_in[l], b_forget[l], conv_w[l], conv_b[l], conv_ln_g[l],
                   conv_ln_b[l], w_out[l], g_mix_post[l], g_ffn_pre[l], w_gate[l], w_up[l],
                   w_down[l], g_ffn_post[l])
    return x
```

```python
import functools
from typing import NamedTuple

import jax
import jax.numpy as jnp
from jax import lax
from jax.experimental import pallas as pl
from jax.experimental.pallas import tpu as pltpu

HEAD_DIM = 64
CONV_WIDTH = 31
EPS = 1e-6
LANES = 128
SUBLANES = 8
BF16_ROWS = 16
HALO_ROWS = 32
CONV_CHUNK = 32
DECAY_SPLITS = 3
VMEM_LIMIT_BYTES = 56 * 1024 * 1024
NEG_BIG = -1e30
LOG2_E = 1.4426950408889634

_f32 = jnp.float32
_bf16 = jnp.bfloat16


def _cparams(sem):
    return pltpu.CompilerParams(dimension_semantics=sem, vmem_limit_bytes=VMEM_LIMIT_BYTES)


def _resident(shape):
    nd = len(shape)
    return pl.BlockSpec(shape, lambda *_: (0,) * nd, pipeline_mode=pl.Buffered(1))


def _rms_scale(x):
    return lax.rsqrt(jnp.mean(x * x, axis=-1, keepdims=True) + EPS)


def _split3(x):
    hi = x.astype(_bf16)
    r1 = x - hi.astype(_f32)
    mid = r1.astype(_bf16)
    lo = (r1 - mid.astype(_f32)).astype(_bf16)
    return hi, mid, lo


def _in_proj_kernel(x_ref, g_ref, wf_ref, wn_ref, wt_ref, bf_ref, u_ref, k_ref, aug_ref, qt_ref, vt_ref,
                    carry_ref, *, d_conv, n_cols):
    @pl.when(pl.program_id(1) == 0)
    def _():
        carry_ref[...] = jnp.zeros_like(carry_ref)

    x = x_ref[...]
    tm = x.shape[0]
    d_attn = qt_ref.shape[0]
    h = (x * _rms_scale(x) * g_ref[...]).astype(_bf16)

    z = jnp.dot(h, wf_ref[...], preferred_element_type=_f32) + bf_ref[...]
    nat = jnp.dot(h, wn_ref[...], preferred_element_type=_f32)
    u_ref[...] = (nat[:, :d_conv] * jax.nn.sigmoid(nat[:, d_conv:2 * d_conv])).astype(u_ref.dtype)
    k_ref[...] = nat[:, 2 * d_conv:].astype(k_ref.dtype)

    log_f = jnp.minimum(z, 0.0) - jnp.log1p(jnp.exp(-jnp.abs(z)))
    lf = log_f * LOG2_E
    row = lax.broadcasted_iota(jnp.int32, (LANES, LANES), 0)
    col = lax.broadcasted_iota(jnp.int32, (LANES, LANES), 1)
    tri = (col <= row).astype(_bf16)
    piece = col % DECAY_SPLITS
    local = []
    for gi in range(tm // LANES):
        hi, mid, lo = _split3(lf[gi * LANES:(gi + 1) * LANES, :])
        local.append(jnp.dot(tri, hi, preferred_element_type=_f32)
                     + jnp.dot(tri, mid, preferred_element_type=_f32)
                     + jnp.dot(tri, lo, preferred_element_type=_f32))

    tr = lax.dot_general(wt_ref[...], h, (((1,), (1,)), ((), ())), preferred_element_type=_f32)
    qt_ref[...] = tr[:d_attn].astype(qt_ref.dtype)
    vt_ref[...] = tr[d_attn:].astype(vt_ref.dtype)

    carry = carry_ref[...]
    for gi in range(tm // LANES):
        c = local[gi] + carry
        nhi, nmid, nlo = _split3(-c)
        aug = jnp.where(piece == 0, nhi, jnp.where(piece == 1, nmid, nlo))
        aug_ref[gi * LANES:(gi + 1) * LANES, :] = jnp.where(col < n_cols, aug, jnp.zeros_like(aug))
        carry = c[LANES - 1:LANES, :]
    carry_ref[...] = carry


def _in_proj(x, g, w_f, w_nat, w_tr, bf_ext, *, d_conv, d_attn, n_cols, tm):
    b, s, d = x.shape
    kern = functools.partial(_in_proj_kernel, d_conv=d_conv, n_cols=n_cols)
    return pl.pallas_call(
        kern,
        grid=(b, s // tm),
        in_specs=[
            pl.BlockSpec((None, tm, d), lambda i, j: (i, j, 0)),
            _resident((1, d)),
            _resident((d, LANES)),
            _resident((d, 2 * d_conv + d_attn)),
            _resident((2 * d_attn, d)),
            _resident((1, LANES)),
        ],
        out_specs=[
            pl.BlockSpec((None, tm, d_conv), lambda i, j: (i, j, 0)),
            pl.BlockSpec((None, tm, d_attn), lambda i, j: (i, j, 0)),
            pl.BlockSpec((None, tm, LANES), lambda i, j: (i, j, 0)),
            pl.BlockSpec((None, d_attn, tm), lambda i, j: (i, 0, j)),
            pl.BlockSpec((None, d_attn, tm), lambda i, j: (i, 0, j)),
        ],
        out_shape=[
            jax.ShapeDtypeStruct((b, s, d_conv), _bf16),
            jax.ShapeDtypeStruct((b, s, d_attn), _bf16),
            jax.ShapeDtypeStruct((b, s, LANES), _bf16),
            jax.ShapeDtypeStruct((b, d_attn, s), _bf16),
            jax.ShapeDtypeStruct((b, d_attn, s), _bf16),
        ],
        scratch_shapes=[pltpu.VMEM((1, LANES), _f32)],
        compiler_params=_cparams(("parallel", "arbitrary")),
        name="in_proj",
    )(x, g, w_f, w_nat, w_tr, bf_ext)


def _attn_kernel(qt_ref, k_ref, aug_ref, vt_ref, o_ref, q_ext_ref, s_ref, mx_ref, p_ref, acc_ref, m_ref,
                 alpha_ref, *, t):
    group = pl.program_id(1)
    qi = pl.program_id(2)
    t0 = qi * t
    hd = HEAD_DIM
    acc_rows = acc_ref.shape[1]
    n_heads = acc_ref.shape[0]
    heads = range(n_heads)

    row = lax.broadcasted_iota(jnp.int32, (LANES, t), 0)
    for hh in heads:
        head = n_heads * group + hh
        lo = (hh % 2) * hd
        qt = qt_ref[(hh // 2) * LANES:(hh // 2 + 1) * LANES, :]
        top = jnp.where((row >= lo) & (row < lo + hd), qt, jnp.zeros_like(qt))
        sel = ((row >= DECAY_SPLITS * head) & (row < DECAY_SPLITS * (head + 1))).astype(_bf16)
        q_ext_ref[hh, 0:LANES, :] = top
        q_ext_ref[hh, LANES:, :] = sel
    acc_ref[...] = jnp.zeros_like(acc_ref)
    m_ref[...] = jnp.full_like(m_ref, NEG_BIG)
    alpha_ref[...] = jnp.ones_like(alpha_ref)
    p_ref[1] = jnp.zeros_like(p_ref[1])
    ones_rows = jnp.ones((acc_rows - hd, t), _bf16)

    def logits_into(slot, s0):
        aug = aug_ref[pl.ds(s0, t), :]
        for hh in heads:
            k_pair = k_ref[pl.ds(s0, t), (hh // 2) * LANES:(hh // 2 + 1) * LANES]
            kk = jnp.concatenate([k_pair, aug], axis=1)
            st = jnp.dot(kk, q_ext_ref[hh], preferred_element_type=_f32)
            s_ref[slot, hh] = st
            mx_ref[slot, hh] = jnp.max(st, axis=0, keepdims=True)

    def softmax_stage(slot, hh, masked):
        def tile():
            st = s_ref[slot, hh]
            if masked:
                valid = (lax.broadcasted_iota(jnp.int32, (t, t), 0)
                         <= lax.broadcasted_iota(jnp.int32, (t, t), 1))
                st = jnp.where(valid, st, NEG_BIG)
            return st
        tile_max = jnp.max(tile(), axis=0, keepdims=True) if masked else mx_ref[slot, hh]
        m_old = m_ref[hh]
        m_new = jnp.maximum(m_old, tile_max)
        m_ref[hh] = m_new
        return jnp.exp2(tile() - m_new).astype(_bf16), jnp.exp2(m_old - m_new)

    def weighted_values(hh, s0, p):
        v_ext = jnp.concatenate([vt_ref[hh * hd:(hh + 1) * hd, pl.ds(s0, t)], ones_rows], axis=0)
        return jnp.dot(v_ext, p, preferred_element_type=_f32)

    def trip(j, slot):
        logits_into(1 - slot, pl.multiple_of((j + 1) * t, t))
        s_prev = pl.multiple_of(jnp.maximum(j - 1, 0) * t, t)
        for hh in heads:
            p, alpha = softmax_stage(slot, hh, masked=False)
            acc_ref[hh] = alpha_ref[hh] * acc_ref[hh] + weighted_values(hh, s_prev, p_ref[1 - slot, hh])
            alpha_ref[hh] = alpha
            p_ref[slot, hh] = p

    def finish(slot):
        s_prev = pl.multiple_of(jnp.maximum(qi - 1, 0) * t, t)
        for hh in heads:
            p, alpha = softmax_stage(slot, hh, masked=True)
            acc = alpha_ref[hh] * acc_ref[hh] + weighted_values(hh, s_prev, p_ref[1 - slot, hh])
            acc = alpha * acc + weighted_values(hh, pl.multiple_of(t0, t), p)
            inv_l = 1.0 / acc[hd:hd + 1, :]
            o_ref[hh * hd:(hh + 1) * hd, :] = (acc[:hd, :] * inv_l).astype(o_ref.dtype)

    logits_into(0, 0)

    def two_trips(i, _):
        trip(2 * i, 0)
        trip(2 * i + 1, 1)
        return 0

    lax.fori_loop(0, lax.shift_right_logical(qi, 1), two_trips, 0)
    odd = (qi & 1) == 1

    @pl.when(odd)
    def _():
        trip(qi - 1, 0)
        finish(1)

    @pl.when(jnp.logical_not(odd))
    def _():
        finish(0)


def _attention(qt, k, aug, vt, *, t, hps):
    b, da, s = qt.shape
    w = hps * HEAD_DIM
    acc_rows = HEAD_DIM + BF16_ROWS
    kern = functools.partial(_attn_kernel, t=t)
    return pl.pallas_call(
        kern,
        grid=(b, da // w, s // t),
        in_specs=[
            pl.BlockSpec((None, w, t), lambda i, g, j: (i, g, j)),
            pl.BlockSpec((None, s, w), lambda i, g, j: (i, 0, g), pipeline_mode=pl.Buffered(1)),
            pl.BlockSpec((None, s, LANES), lambda i, g, j: (i, 0, 0), pipeline_mode=pl.Buffered(1)),
            pl.BlockSpec((None, w, s), lambda i, g, j: (i, g, 0), pipeline_mode=pl.Buffered(1)),
        ],
        out_specs=pl.BlockSpec((None, w, t), lambda i, g, j: (i, g, j)),
        out_shape=jax.ShapeDtypeStruct((b, da, s), _bf16),
        scratch_shapes=[
            pltpu.VMEM((hps, 2 * LANES, t), _bf16),
            pltpu.VMEM((2, hps, t, t), _f32),
            pltpu.VMEM((2, hps, 1, t), _f32),
            pltpu.VMEM((2, hps, t, t), _bf16),
            pltpu.VMEM((hps, acc_rows, t), _f32),
            pltpu.VMEM((hps, 1, t), _f32),
            pltpu.VMEM((hps, 1, t), _f32),
        ],
        compiler_params=_cparams(("parallel", "parallel", "arbitrary")),
        name="attention",
    )(qt, k, aug, vt)


def _conv_tile(u_cur, halo, w_ref, cb_ref, lg_ref, lb_ref, win_ref, out_ref):
    tc = u_cur.shape[0]
    win_ref[0, 0:HALO_ROWS, :] = halo
    win_ref[0, HALO_ROWS:, :] = u_cur.astype(_f32)
    n_shift = tc + HALO_ROWS - SUBLANES
    for r in range(1, SUBLANES):
        win_ref[r, 0:n_shift, :] = win_ref[0, r:r + n_shift, :]
    lead = HALO_ROWS - (CONV_WIDTH - 1)
    groups = (CONV_CHUNK // SUBLANES, SUBLANES, cb_ref.shape[1])
    for r0 in range(0, tc, CONV_CHUNK):
        acc = jnp.broadcast_to(cb_ref[...], groups)
        for k in range(CONV_WIDTH):
            phase, base = (lead + k) % SUBLANES, (lead + k) // SUBLANES * SUBLANES
            tap = win_ref[phase, r0 + base:r0 + base + CONV_CHUNK, :].reshape(groups)
            acc = acc + w_ref[k] * tap
        acc = acc.reshape(CONV_CHUNK, groups[2])
        mu = jnp.mean(acc, axis=-1, keepdims=True)
        cen = acc - mu
        var = jnp.mean(cen * cen, axis=-1, keepdims=True)
        y = cen * lax.rsqrt(var + EPS) * lg_ref[...] + lb_ref[...]
        out_ref[r0:r0 + CONV_CHUNK, :] = (y * jax.nn.sigmoid(y)).astype(out_ref.dtype)


def _mix_ffn_kernel(x_ref, at_ref, u0_ref, un_ref, hn_ref, cw_ref, cb_ref, lg_ref, lb_ref, woc_ref,
                    woa_ref, gpost_ref, gpre_ref, wg_ref, wu_ref, wd_ref, gout_ref, o_ref, win_ref,
                    conv_ref, conv_next_ref):
    conv_args = (cw_ref, cb_ref, lg_ref, lb_ref, win_ref, conv_ref)
    last_in_seq = pl.program_id(1) == pl.num_programs(1) - 1

    @pl.when((pl.program_id(0) == 0) & (pl.program_id(1) == 0))
    def _():
        _conv_tile(u0_ref[...], jnp.zeros((HALO_ROWS, u0_ref.shape[1]), _f32), *conv_args)

    halo = hn_ref[...].astype(_f32)
    _conv_tile(un_ref[...], jnp.where(last_in_seq, jnp.zeros_like(halo), halo),
               cw_ref, cb_ref, lg_ref, lb_ref, win_ref, conv_next_ref)

    tm = x_ref.shape[0]
    halves = [slice(0, tm // 2), slice(tm // 2, tm)]
    mix = [jnp.dot(conv_ref[r, :], woc_ref[...], preferred_element_type=_f32)
           + lax.dot_general(at_ref[:, r], woa_ref[...], (((0,), (0,)), ((), ())),
                             preferred_element_type=_f32) for r in halves]
    x1, gate, up = [], [], []
    for r, mx in zip(halves, mix):
        x1.append(x_ref[r, :] + mx * _rms_scale(mx) * gpost_ref[...])
        h = (x1[-1] * _rms_scale(x1[-1]) * gpre_ref[...]).astype(_bf16)
        gate.append(jnp.dot(h, wg_ref[...], preferred_element_type=_f32))
        up.append(jnp.dot(h, wu_ref[...], preferred_element_type=_f32))
    for r, x1_r, g_r, u_r in zip(halves, x1, gate, up):
        act = (g_r * jax.nn.sigmoid(g_r) * u_r).astype(_bf16)
        y = jnp.dot(act, wd_ref[...], preferred_element_type=_f32)
        o_ref[r, :] = x1_r + y * _rms_scale(y) * gout_ref[...]

    conv_ref[...] = conv_next_ref[...]


def _mix_ffn(x, u, attn_t, conv_w, conv_b, ln_g, ln_b, wo_c, wo_a, g_post, g_pre, w_gate, w_up, w_down,
             g_out, *, tm):
    b, s, d = x.shape
    dc = u.shape[2]
    da = attn_t.shape[1]
    dff = w_gate.shape[1]
    nt = s // tm
    halo_blocks = tm // HALO_ROWS

    def next_tile(i, j):
        flat = jnp.minimum(i * nt + j + 1, b * nt - 1)
        return flat // nt, flat % nt

    def u_next(i, j):
        ni, nj = next_tile(i, j)
        return ni, nj, 0

    def halo_next(i, j):
        ni, nj = next_tile(i, j)
        return ni, jnp.maximum(nj * halo_blocks - 1, 0), 0

    return pl.pallas_call(
        _mix_ffn_kernel,
        grid=(b, nt),
        in_specs=[
            pl.BlockSpec((None, tm, d), lambda i, j: (i, j, 0)),
            pl.BlockSpec((None, da, tm), lambda i, j: (i, 0, j)),
            pl.BlockSpec((None, tm, dc), lambda i, j: (0, 0, 0), pipeline_mode=pl.Buffered(1)),
            pl.BlockSpec((None, tm, dc), u_next),
            pl.BlockSpec((None, HALO_ROWS, dc), halo_next),
            _resident((CONV_WIDTH, SUBLANES, dc)),
            _resident((SUBLANES, dc)),
            _resident((1, dc)),
            _resident((1, dc)),
            _resident((dc, d)),
            _resident((da, d)),
            _resident((1, d)),
            _resident((1, d)),
            _resident((d, dff)),
            _resident((d, dff)),
            _resident((dff, d)),
            _resident((1, d)),
        ],
        out_specs=pl.BlockSpec((None, tm, d), lambda i, j: (i, j, 0)),
        out_shape=jax.ShapeDtypeStruct((b, s, d), x.dtype),
        scratch_shapes=[
            pltpu.VMEM((SUBLANES, tm + HALO_ROWS, dc), _f32),
            pltpu.VMEM((tm, dc), _bf16),
            pltpu.VMEM((tm, dc), _bf16),
        ],
        compiler_params=_cparams(("arbitrary", "arbitrary")),
        name="mix_ffn",
    )(x, attn_t, u, u, u, conv_w, conv_b, ln_g, ln_b, wo_c, wo_a, g_post, g_pre, w_gate, w_up, w_down, g_out)


class _Tiles(NamedTuple):
    proj_rows: int
    attn: int
    ffn_rows: int


def _tile_sizes(s):
    tiles = _Tiles(proj_rows=min(1024, s), attn=min(512, s), ffn_rows=min(512, s))
    assert all(s % n == 0 and n % LANES == 0 for n in tiles) and tiles.ffn_rows % CONV_CHUNK == 0
    return tiles


def _layer(x, g_mix_pre, w_in, b_forget, conv_w, conv_b, conv_ln_g, conv_ln_b, w_out, g_mix_post,
           g_ffn_pre, w_gate, w_up, w_down, g_ffn_post):
    b, s, d = x.shape
    d_conv = conv_w.shape[1]
    n_heads = b_forget.shape[0]
    d_attn = n_heads * HEAD_DIM
    n_cols = DECAY_SPLITS * n_heads
    assert 2 * d_conv + 3 * d_attn + n_heads == w_in.shape[1]
    assert n_cols <= LANES and d_attn % LANES == 0

    tiles = _tile_sizes(s)

    o_q, o_k, o_v, o_f = 2 * d_conv, 2 * d_conv + d_attn, 2 * d_conv + 2 * d_attn, 2 * d_conv + 3 * d_attn
    w_f = jnp.repeat(w_in[:, o_f:], DECAY_SPLITS, axis=1)
    w_f = jnp.pad(w_f, ((0, 0), (0, LANES - n_cols))).astype(_bf16)
    w_nat = jnp.concatenate([w_in[:, :o_q], w_in[:, o_k:o_v]], axis=1).astype(_bf16)
    w_tr = jnp.concatenate([w_in[:, o_q:o_k] * (HEAD_DIM ** -0.5 * LOG2_E), w_in[:, o_v:o_f]], axis=1).T.astype(_bf16)
    bf_ext = jnp.pad(jnp.repeat(b_forget.astype(_f32), DECAY_SPLITS), (0, LANES - n_cols))[None, :]

    row = lambda v: v.astype(_f32)[None, :]
    u, k, aug, qt, vt = _in_proj(x, row(g_mix_pre), w_f, w_nat, w_tr, bf_ext, d_conv=d_conv, d_attn=d_attn,
                                 n_cols=n_cols, tm=tiles.proj_rows)
    attn_t = _attention(qt, k, aug, vt, t=tiles.attn, hps=n_heads)
    over_sublanes = lambda v: jnp.broadcast_to(v.astype(_f32)[..., None, :], v.shape[:-1] + (SUBLANES, v.shape[-1]))
    return _mix_ffn(x, u, attn_t, over_sublanes(conv_w), over_sublanes(conv_b), row(conv_ln_g), row(conv_ln_b),
                    w_out[:d_conv].astype(_bf16), w_out[d_conv:].astype(_bf16), row(g_mix_post),
                    row(g_ffn_pre), w_gate.astype(_bf16), w_up.astype(_bf16), w_down.astype(_bf16),
                    row(g_ffn_post), tm=tiles.ffn_rows)


def kernel(x, g_mix_pre, w_in, b_forget, conv_w, conv_b, conv_ln_g, conv_ln_b, w_out, g_mix_post,
           g_ffn_pre, w_gate, w_up, w_down, g_ffn_post):
    for l in range(g_mix_pre.shape[0]):
        x = _layer(x, g_mix_pre[l], w_in[l], b_forget[l], conv_w[l], conv_b[l], conv_ln_g[l],
                   conv_ln_b[l], w_out[l], g_mix_post[l], g_ffn_pre[l], w_gate[l], w_up[l],
                   w_down[l], g_ffn_post[l])
    return x
```

```python
import functools
from typing import NamedTuple

import jax
import jax.numpy as jnp
from jax import lax
from jax.experimental import pallas as pl
from jax.experimental.pallas import tpu as pltpu

HEAD_DIM = 64
CONV_WIDTH = 31
EPS = 1e-6
LANES = 128
SUBLANES = 8
BF16_ROWS = 16
HALO_ROWS = 32
CONV_CHUNK = 32
DECAY_SPLITS = 3
VMEM_LIMIT_BYTES = 56 * 1024 * 1024
NEG_BIG = -1e30
LOG2_E = 1.4426950408889634

_f32 = jnp.float32
_bf16 = jnp.bfloat16


def _cparams(sem):
    return pltpu.CompilerParams(dimension_semantics=sem, vmem_limit_bytes=VMEM_LIMIT_BYTES)


def _resident(shape):
    nd = len(shape)
    return pl.BlockSpec(shape, lambda *_: (0,) * nd, pipeline_mode=pl.Buffered(1))


def _rms_scale(x):
    return lax.rsqrt(jnp.mean(x * x, axis=-1, keepdims=True) + EPS)


def _split3(x):
    hi = x.astype(_bf16)
    r1 = x - hi.astype(_f32)
    mid = r1.astype(_bf16)
    lo = (r1 - mid.astype(_f32)).astype(_bf16)
    return hi, mid, lo


def _in_proj_kernel(x_ref, g_ref, wf_ref, wn_ref, wt_ref, bf_ref, cw_ref, cb_ref, lg_ref, lb_ref,
                    conv_ref, k_ref, aug_ref, qt_ref, vt_ref, carry_ref, u_ref, halo_ref, win_ref, *, d_conv,
                    n_cols):
    j = pl.program_id(1)
    repeat = j == pl.num_programs(1) - 1

    @pl.when(j == 0)
    def _():
        carry_ref[...] = jnp.zeros_like(carry_ref)
        u_ref[...] = jnp.zeros_like(u_ref)
        halo_ref[...] = jnp.zeros_like(halo_ref)

    _conv_tile(u_ref[...], halo_ref[...], cw_ref, cb_ref, lg_ref, lb_ref, win_ref, conv_ref)

    x = x_ref[...]
    tm = x.shape[0]
    d_attn = qt_ref.shape[0]
    h = (x * _rms_scale(x) * g_ref[...]).astype(_bf16)

    z = jnp.dot(h, wf_ref[...], preferred_element_type=_f32) + bf_ref[...]
    nat = jnp.dot(h, wn_ref[...], preferred_element_type=_f32)
    k_ref[...] = nat[:, 2 * d_conv:].astype(k_ref.dtype)
    halo_ref[...] = u_ref[tm - HALO_ROWS:, :].astype(_f32)
    u_ref[...] = (nat[:, :d_conv] * jax.nn.sigmoid(nat[:, d_conv:2 * d_conv])).astype(_bf16)

    log_f = jnp.minimum(z, 0.0) - jnp.log1p(jnp.exp(-jnp.abs(z)))
    lf = log_f * LOG2_E
    row = lax.broadcasted_iota(jnp.int32, (LANES, LANES), 0)
    col = lax.broadcasted_iota(jnp.int32, (LANES, LANES), 1)
    tri = (col <= row).astype(_bf16)
    piece = col % DECAY_SPLITS
    local = []
    for gi in range(tm // LANES):
        hi, mid, lo = _split3(lf[gi * LANES:(gi + 1) * LANES, :])
        local.append(jnp.dot(tri, hi, preferred_element_type=_f32)
                     + jnp.dot(tri, mid, preferred_element_type=_f32)
                     + jnp.dot(tri, lo, preferred_element_type=_f32))

    tr = lax.dot_general(wt_ref[...], h, (((1,), (1,)), ((), ())), preferred_element_type=_f32)
    qt_ref[...] = tr[:d_attn].astype(qt_ref.dtype)
    vt_ref[...] = tr[d_attn:].astype(vt_ref.dtype)

    carry = jnp.where(repeat, carry_ref[1:2, :], carry_ref[0:1, :])
    carry_ref[1:2, :] = carry
    for gi in range(tm // LANES):
        c = local[gi] + carry
        nhi, nmid, nlo = _split3(-c)
        aug = jnp.where(piece == 0, nhi, jnp.where(piece == 1, nmid, nlo))
        aug_ref[gi * LANES:(gi + 1) * LANES, :] = jnp.where(col < n_cols, aug, jnp.zeros_like(aug))
        carry = c[LANES - 1:LANES, :]
    carry_ref[0:1, :] = carry


def _in_proj(x, g, w_f, w_nat, w_tr, bf_ext, conv_w, conv_b, ln_g, ln_b, *, d_conv, d_attn, n_cols, tm):
    b, s, d = x.shape
    kern = functools.partial(_in_proj_kernel, d_conv=d_conv, n_cols=n_cols)
    nt = s // tm
    cur = lambda j: jnp.minimum(j, nt - 1)
    return pl.pallas_call(
        kern,
        grid=(b, nt + 1),
        in_specs=[
            pl.BlockSpec((None, tm, d), lambda i, j: (i, cur(j), 0)),
            _resident((1, d)),
            _resident((d, LANES)),
            _resident((d, 2 * d_conv + d_attn)),
            _resident((2 * d_attn, d)),
            _resident((1, LANES)),
            _resident((CONV_WIDTH, SUBLANES, d_conv)),
            _resident((SUBLANES, d_conv)),
            _resident((1, d_conv)),
            _resident((1, d_conv)),
        ],
        out_specs=[
            pl.BlockSpec((None, tm, d_conv), lambda i, j: (i, jnp.maximum(j - 1, 0), 0)),
            pl.BlockSpec((None, tm, d_attn), lambda i, j: (i, cur(j), 0)),
            pl.BlockSpec((None, tm, LANES), lambda i, j: (i, cur(j), 0)),
            pl.BlockSpec((None, d_attn, tm), lambda i, j: (i, 0, cur(j))),
            pl.BlockSpec((None, d_attn, tm), lambda i, j: (i, 0, cur(j))),
        ],
        out_shape=[
            jax.ShapeDtypeStruct((b, s, d_conv), _bf16),
            jax.ShapeDtypeStruct((b, s, d_attn), _bf16),
            jax.ShapeDtypeStruct((b, s, LANES), _bf16),
            jax.ShapeDtypeStruct((b, d_attn, s), _bf16),
            jax.ShapeDtypeStruct((b, d_attn, s), _bf16),
        ],
        scratch_shapes=[
            pltpu.VMEM((2, LANES), _f32),
            pltpu.VMEM((tm, d_conv), _bf16),
            pltpu.VMEM((HALO_ROWS, d_conv), _f32),
            pltpu.VMEM((SUBLANES, tm + HALO_ROWS, d_conv), _f32),
        ],
        compiler_params=_cparams(("parallel", "arbitrary")),
        name="in_proj",
    )(x, g, w_f, w_nat, w_tr, bf_ext, conv_w, conv_b, ln_g, ln_b)


def _attn_kernel(qt_ref, k_ref, aug_ref, vt_ref, o_ref, q_ext_ref, s_ref, mx_ref, p_ref, acc_ref, m_ref,
                 alpha_ref, *, t):
    group = pl.program_id(1)
    qi = pl.program_id(2)
    t0 = qi * t
    hd = HEAD_DIM
    acc_rows = acc_ref.shape[1]
    n_heads = acc_ref.shape[0]
    heads = range(n_heads)

    row = lax.broadcasted_iota(jnp.int32, (LANES, t), 0)
    for hh in heads:
        head = n_heads * group + hh
        lo = (hh % 2) * hd
        qt = qt_ref[(hh // 2) * LANES:(hh // 2 + 1) * LANES, :]
        top = jnp.where((row >= lo) & (row < lo + hd), qt, jnp.zeros_like(qt))
        sel = ((row >= DECAY_SPLITS * head) & (row < DECAY_SPLITS * (head + 1))).astype(_bf16)
        q_ext_ref[hh, 0:LANES, :] = top
        q_ext_ref[hh, LANES:, :] = sel
    acc_ref[...] = jnp.zeros_like(acc_ref)
    m_ref[...] = jnp.full_like(m_ref, NEG_BIG)
    alpha_ref[...] = jnp.ones_like(alpha_ref)
    p_ref[1] = jnp.zeros_like(p_ref[1])
    ones_rows = jnp.ones((acc_rows - hd, t), _bf16)

    def logits_into(slot, s0):
        aug = aug_ref[pl.ds(s0, t), :]
        for hh in heads:
            k_pair = k_ref[pl.ds(s0, t), (hh // 2) * LANES:(hh // 2 + 1) * LANES]
            kk = jnp.concatenate([k_pair, aug], axis=1)
            st = jnp.dot(kk, q_ext_ref[hh], preferred_element_type=_f32)
            s_ref[slot, hh] = st
            mx_ref[slot, hh] = jnp.max(st, axis=0, keepdims=True)

    def softmax_stage(slot, hh, masked):
        def tile():
            st = s_ref[slot, hh]
            if masked:
                valid = (lax.broadcasted_iota(jnp.int32, (t, t), 0)
                         <= lax.broadcasted_iota(jnp.int32, (t, t), 1))
                st = jnp.where(valid, st, NEG_BIG)
            return st
        tile_max = jnp.max(tile(), axis=0, keepdims=True) if masked else mx_ref[slot, hh]
        m_old = m_ref[hh]
        m_new = jnp.maximum(m_old, tile_max)
        m_ref[hh] = m_new
        return jnp.exp2(tile() - m_new).astype(_bf16), jnp.exp2(m_old - m_new)

    def weighted_values(hh, s0, p):
        v_ext = jnp.concatenate([vt_ref[hh * hd:(hh + 1) * hd, pl.ds(s0, t)], ones_rows], axis=0)
        return jnp.dot(v_ext, p, preferred_element_type=_f32)

    def trip(j, slot):
        logits_into(1 - slot, pl.multiple_of((j + 1) * t, t))
        s_prev = pl.multiple_of(jnp.maximum(j - 1, 0) * t, t)
        for hh in heads:
            p, alpha = softmax_stage(slot, hh, masked=False)
            acc_ref[hh] = alpha_ref[hh] * acc_ref[hh] + weighted_values(hh, s_prev, p_ref[1 - slot, hh])
            alpha_ref[hh] = alpha
            p_ref[slot, hh] = p

    def finish(slot):
        s_prev = pl.multiple_of(jnp.maximum(qi - 1, 0) * t, t)
        for hh in heads:
            p, alpha = softmax_stage(slot, hh, masked=True)
            acc = alpha_ref[hh] * acc_ref[hh] + weighted_values(hh, s_prev, p_ref[1 - slot, hh])
            acc = alpha * acc + weighted_values(hh, pl.multiple_of(t0, t), p)
            inv_l = 1.0 / acc[hd:hd + 1, :]
            o_ref[hh * hd:(hh + 1) * hd, :] = (acc[:hd, :] * inv_l).astype(o_ref.dtype)

    logits_into(0, 0)

    def two_trips(i, _):
        trip(2 * i, 0)
        trip(2 * i + 1, 1)
        return 0

    lax.fori_loop(0, lax.shift_right_logical(qi, 1), two_trips, 0)
    odd = (qi & 1) == 1

    @pl.when(odd)
    def _():
        trip(qi - 1, 0)
        finish(1)

    @pl.when(jnp.logical_not(odd))
    def _():
        finish(0)


def _attention(qt, k, aug, vt, *, t, hps):
    b, da, s = qt.shape
    w = hps * HEAD_DIM
    acc_rows = HEAD_DIM + BF16_ROWS
    kern = functools.partial(_attn_kernel, t=t)
    return pl.pallas_call(
        kern,
        grid=(b, da // w, s // t),
        in_specs=[
            pl.BlockSpec((None, w, t), lambda i, g, j: (i, g, j)),
            pl.BlockSpec((None, s, w), lambda i, g, j: (i, 0, g), pipeline_mode=pl.Buffered(1)),
            pl.BlockSpec((None, s, LANES), lambda i, g, j: (i, 0, 0), pipeline_mode=pl.Buffered(1)),
            pl.BlockSpec((None, w, s), lambda i, g, j: (i, g, 0), pipeline_mode=pl.Buffered(1)),
        ],
        out_specs=pl.BlockSpec((None, w, t), lambda i, g, j: (i, g, j)),
        out_shape=jax.ShapeDtypeStruct((b, da, s), _bf16),
        scratch_shapes=[
            pltpu.VMEM((hps, 2 * LANES, t), _bf16),
            pltpu.VMEM((2, hps, t, t), _f32),
            pltpu.VMEM((2, hps, 1, t), _f32),
            pltpu.VMEM((2, hps, t, t), _bf16),
            pltpu.VMEM((hps, acc_rows, t), _f32),
            pltpu.VMEM((hps, 1, t), _f32),
            pltpu.VMEM((hps, 1, t), _f32),
        ],
        compiler_params=_cparams(("parallel", "parallel", "arbitrary")),
        name="attention",
    )(qt, k, aug, vt)


def _conv_tile(u_cur, halo, w_ref, cb_ref, lg_ref, lb_ref, win_ref, out_ref):
    tc = u_cur.shape[0]
    win_ref[0, 0:HALO_ROWS, :] = halo
    win_ref[0, HALO_ROWS:, :] = u_cur.astype(_f32)
    n_shift = tc + HALO_ROWS - SUBLANES
    for r in range(1, SUBLANES):
        win_ref[r, 0:n_shift, :] = win_ref[0, r:r + n_shift, :]
    lead = HALO_ROWS - (CONV_WIDTH - 1)
    groups = (CONV_CHUNK // SUBLANES, SUBLANES, cb_ref.shape[1])
    for r0 in range(0, tc, CONV_CHUNK):
        acc = jnp.broadcast_to(cb_ref[...], groups)
        for k in range(CONV_WIDTH):
            phase, base = (lead + k) % SUBLANES, (lead + k) // SUBLANES * SUBLANES
            tap = win_ref[phase, r0 + base:r0 + base + CONV_CHUNK, :].reshape(groups)
            acc = acc + w_ref[k] * tap
        acc = acc.reshape(CONV_CHUNK, groups[2])
        mu = jnp.mean(acc, axis=-1, keepdims=True)
        cen = acc - mu
        var = jnp.mean(cen * cen, axis=-1, keepdims=True)
        y = cen * lax.rsqrt(var + EPS) * lg_ref[...] + lb_ref[...]
        out_ref[r0:r0 + CONV_CHUNK, :] = (y * jax.nn.sigmoid(y)).astype(out_ref.dtype)


def _mix_ffn_kernel(x_ref, conv_ref, at_ref, woc_ref, woa_ref, gpost_ref, gpre_ref, wg_ref, wu_ref, wd_ref,
                    gout_ref, o_ref):
    tm = x_ref.shape[0]
    halves = [slice(0, tm // 2), slice(tm // 2, tm)]
    mix = [jnp.dot(conv_ref[r, :], woc_ref[...], preferred_element_type=_f32)
           + lax.dot_general(at_ref[:, r], woa_ref[...], (((0,), (0,)), ((), ())),
                             preferred_element_type=_f32) for r in halves]
    x1, gate, up = [], [], []
    for r, mx in zip(halves, mix):
        x1.append(x_ref[r, :] + mx * _rms_scale(mx) * gpost_ref[...])
        h = (x1[-1] * _rms_scale(x1[-1]) * gpre_ref[...]).astype(_bf16)
        gate.append(jnp.dot(h, wg_ref[...], preferred_element_type=_f32))
        up.append(jnp.dot(h, wu_ref[...], preferred_element_type=_f32))
    for r, x1_r, g_r, u_r in zip(halves, x1, gate, up):
        act = (g_r * jax.nn.sigmoid(g_r) * u_r).astype(_bf16)
        y = jnp.dot(act, wd_ref[...], preferred_element_type=_f32)
        o_ref[r, :] = x1_r + y * _rms_scale(y) * gout_ref[...]


def _mix_ffn(x, conv_out, attn_t, wo_c, wo_a, g_post, g_pre, w_gate, w_up, w_down, g_out, *, tm):
    b, s, d = x.shape
    dc = conv_out.shape[2]
    da = attn_t.shape[1]
    dff = w_gate.shape[1]
    return pl.pallas_call(
        _mix_ffn_kernel,
        grid=(b, s // tm),
        in_specs=[
            pl.BlockSpec((None, tm, d), lambda i, j: (i, j, 0)),
            pl.BlockSpec((None, tm, dc), lambda i, j: (i, j, 0)),
            pl.BlockSpec((None, da, tm), lambda i, j: (i, 0, j)),
            _resident((dc, d)),
            _resident((da, d)),
            _resident((1, d)),
            _resident((1, d)),
            _resident((d, dff)),
            _resident((d, dff)),
            _resident((dff, d)),
            _resident((1, d)),
        ],
        out_specs=pl.BlockSpec((None, tm, d), lambda i, j: (i, j, 0)),
        out_shape=jax.ShapeDtypeStruct((b, s, d), x.dtype),
        compiler_params=_cparams(("parallel", "parallel")),
        name="mix_ffn",
    )(x, conv_out, attn_t, wo_c, wo_a, g_post, g_pre, w_gate, w_up, w_down, g_out)


class _Tiles(NamedTuple):
    proj_rows: int
    attn: int
    ffn_rows: int


def _tile_sizes(s):
    tiles = _Tiles(proj_rows=min(1024, s), attn=min(512, s), ffn_rows=min(512, s))
    assert all(s % n == 0 and n % LANES == 0 for n in tiles) and tiles.ffn_rows % CONV_CHUNK == 0
    return tiles


def _layer(x, g_mix_pre, w_in, b_forget, conv_w, conv_b, conv_ln_g, conv_ln_b, w_out, g_mix_post,
           g_ffn_pre, w_gate, w_up, w_down, g_ffn_post):
    b, s, d = x.shape
    d_conv = conv_w.shape[1]
    n_heads = b_forget.shape[0]
    d_attn = n_heads * HEAD_DIM
    n_cols = DECAY_SPLITS * n_heads
    assert 2 * d_conv + 3 * d_attn + n_heads == w_in.shape[1]
    assert n_cols <= LANES and d_attn % LANES == 0

    tiles = _tile_sizes(s)

    o_q, o_k, o_v, o_f = 2 * d_conv, 2 * d_conv + d_attn, 2 * d_conv + 2 * d_attn, 2 * d_conv + 3 * d_attn
    w_f = jnp.repeat(w_in[:, o_f:], DECAY_SPLITS, axis=1)
    w_f = jnp.pad(w_f, ((0, 0), (0, LANES - n_cols))).astype(_bf16)
    w_nat = jnp.concatenate([w_in[:, :o_q], w_in[:, o_k:o_v]], axis=1).astype(_bf16)
    w_tr = jnp.concatenate([w_in[:, o_q:o_k] * (HEAD_DIM ** -0.5 * LOG2_E), w_in[:, o_v:o_f]], axis=1).T.astype(_bf16)
    bf_ext = jnp.pad(jnp.repeat(b_forget.astype(_f32), DECAY_SPLITS), (0, LANES - n_cols))[None, :]

    row = lambda v: v.astype(_f32)[None, :]
    over_sublanes = lambda v: jnp.broadcast_to(v.astype(_f32)[..., None, :], v.shape[:-1] + (SUBLANES, v.shape[-1]))
    conv_out, k, aug, qt, vt = _in_proj(x, row(g_mix_pre), w_f, w_nat, w_tr, bf_ext, over_sublanes(conv_w),
                                        over_sublanes(conv_b), row(conv_ln_g), row(conv_ln_b), d_conv=d_conv,
                                        d_attn=d_attn, n_cols=n_cols, tm=tiles.proj_rows)
    attn_t = _attention(qt, k, aug, vt, t=tiles.attn, hps=n_heads)
    return _mix_ffn(x, conv_out, attn_t, w_out[:d_conv].astype(_bf16), w_out[d_conv:].astype(_bf16), row(g_mix_post),
                    row(g_ffn_pre), w_gate.astype(_bf16), w_up.astype(_bf16), w_down.astype(_bf16),
                    row(g_ffn_post), tm=tiles.ffn_rows)


def kernel(x, g_mix_pre, w_in, b_forget, conv_w, conv_b, conv_ln_g, conv_ln_b, w_out, g_mix_post,
           g_ffn_pre, w_gate, w_up, w_down, g_ffn_post):
    for l in range(g_mix_pre.shape[0]):
        x = _layer(x, g_mix_pre[l], w_in[l], b_forget[l], conv_w[l], conv_b[l], conv_ln_g[l],
                   conv_ln_b[l], w_out[l], g_mix_post[l], g_ffn_pre[l], w_gate[l], w_up[l],
                   w_down[l], g_ffn_post[l])
    return x
```

```python
import functools
from typing import NamedTuple

import jax
import jax.numpy as jnp
from jax import lax
from jax.experimental import pallas as pl
from jax.experimental.pallas import tpu as pltpu

HEAD_DIM = 64
CONV_WIDTH = 31
EPS = 1e-6
LANES = 128
SUBLANES = 8
BF16_ROWS = 16
HALO_ROWS = 32
CONV_CHUNK = 16
DECAY_SPLITS = 3
VMEM_LIMIT_BYTES = 56 * 1024 * 1024
NEG_BIG = -1e30
LOG2_E = 1.4426950408889634

_f32 = jnp.float32
_bf16 = jnp.bfloat16


def _cparams(sem):
    return pltpu.CompilerParams(dimension_semantics=sem, vmem_limit_bytes=VMEM_LIMIT_BYTES)


def _resident(shape):
    nd = len(shape)
    return pl.BlockSpec(shape, lambda *_: (0,) * nd, pipeline_mode=pl.Buffered(1))


def _rms_scale(x):
    return lax.rsqrt(jnp.mean(x * x, axis=-1, keepdims=True) + EPS)


def _split3(x):
    hi = x.astype(_bf16)
    r1 = x - hi.astype(_f32)
    mid = r1.astype(_bf16)
    lo = (r1 - mid.astype(_f32)).astype(_bf16)
    return hi, mid, lo


def _in_proj_kernel(x_ref, g_ref, wf_ref, wn_ref, wt_ref, bf_ref, cw_ref, cb_ref, lg_ref, lb_ref,
                    conv_ref, k_ref, aug_ref, qt_ref, vt_ref, carry_ref, u_ref, halo_ref, win_ref, *, d_conv,
                    n_cols):
    j = pl.program_id(1)
    repeat = j == pl.num_programs(1) - 1

    @pl.when(j == 0)
    def _():
        carry_ref[...] = jnp.zeros_like(carry_ref)
        u_ref[...] = jnp.zeros_like(u_ref)
        halo_ref[...] = jnp.zeros_like(halo_ref)

    _conv_tile(u_ref[...], halo_ref[...], cw_ref, cb_ref, lg_ref, lb_ref, win_ref, conv_ref)

    x = x_ref[...]
    tm = x.shape[0]
    d_attn = qt_ref.shape[0]
    h = (x * _rms_scale(x) * g_ref[...]).astype(_bf16)

    z = jnp.dot(h, wf_ref[...], preferred_element_type=_f32) + bf_ref[...]
    nat = jnp.dot(h, wn_ref[...], preferred_element_type=_f32)
    k_ref[...] = nat[:, 2 * d_conv:].astype(k_ref.dtype)
    halo_ref[...] = u_ref[tm - HALO_ROWS:, :].astype(_f32)
    u_ref[...] = (nat[:, :d_conv] * jax.nn.sigmoid(nat[:, d_conv:2 * d_conv])).astype(_bf16)

    log_f = jnp.minimum(z, 0.0) - jnp.log1p(jnp.exp(-jnp.abs(z)))
    lf = log_f * LOG2_E
    row = lax.broadcasted_iota(jnp.int32, (LANES, LANES), 0)
    col = lax.broadcasted_iota(jnp.int32, (LANES, LANES), 1)
    tri = (col <= row).astype(_bf16)
    piece = col % DECAY_SPLITS
    local = []
    for gi in range(tm // LANES):
        hi, mid, lo = _split3(lf[gi * LANES:(gi + 1) * LANES, :])
        local.append(jnp.dot(tri, hi, preferred_element_type=_f32)
                     + jnp.dot(tri, mid, preferred_element_type=_f32)
                     + jnp.dot(tri, lo, preferred_element_type=_f32))

    tr = lax.dot_general(wt_ref[...], h, (((1,), (1,)), ((), ())), preferred_element_type=_f32)
    qt_ref[...] = tr[:d_attn].astype(qt_ref.dtype)
    vt_ref[...] = tr[d_attn:].astype(vt_ref.dtype)

    carry = jnp.where(repeat, carry_ref[1:2, :], carry_ref[0:1, :])
    carry_ref[1:2, :] = carry
    for gi in range(tm // LANES):
        c = local[gi] + carry
        nhi, nmid, nlo = _split3(-c)
        aug = jnp.where(piece == 0, nhi, jnp.where(piece == 1, nmid, nlo))
        aug_ref[gi * LANES:(gi + 1) * LANES, :] = jnp.where(col < n_cols, aug, jnp.zeros_like(aug))
        carry = c[LANES - 1:LANES, :]
    carry_ref[0:1, :] = carry


def _in_proj(x, g, w_f, w_nat, w_tr, bf_ext, conv_w, conv_b, ln_g, ln_b, *, d_conv, d_attn, n_cols, tm):
    b, s, d = x.shape
    kern = functools.partial(_in_proj_kernel, d_conv=d_conv, n_cols=n_cols)
    nt = s // tm
    cur = lambda j: jnp.minimum(j, nt - 1)
    return pl.pallas_call(
        kern,
        grid=(b, nt + 1),
        in_specs=[
            pl.BlockSpec((None, tm, d), lambda i, j: (i, cur(j), 0)),
            _resident((1, d)),
            _resident((d, LANES)),
            _resident((d, 2 * d_conv + d_attn)),
            _resident((2 * d_attn, d)),
            _resident((1, LANES)),
            _resident((CONV_WIDTH, SUBLANES, d_conv)),
            _resident((SUBLANES, d_conv)),
            _resident((1, d_conv)),
            _resident((1, d_conv)),
        ],
        out_specs=[
            pl.BlockSpec((None, tm, d_conv), lambda i, j: (i, jnp.maximum(j - 1, 0), 0)),
            pl.BlockSpec((None, tm, d_attn), lambda i, j: (i, cur(j), 0)),
            pl.BlockSpec((None, tm, LANES), lambda i, j: (i, cur(j), 0)),
            pl.BlockSpec((None, d_attn, tm), lambda i, j: (i, 0, cur(j))),
            pl.BlockSpec((None, d_attn, tm), lambda i, j: (i, 0, cur(j))),
        ],
        out_shape=[
            jax.ShapeDtypeStruct((b, s, d_conv), _bf16),
            jax.ShapeDtypeStruct((b, s, d_attn), _bf16),
            jax.ShapeDtypeStruct((b, s, LANES), _bf16),
            jax.ShapeDtypeStruct((b, d_attn, s), _bf16),
            jax.ShapeDtypeStruct((b, d_attn, s), _bf16),
        ],
        scratch_shapes=[
            pltpu.VMEM((2, LANES), _f32),
            pltpu.VMEM((tm, d_conv), _bf16),
            pltpu.VMEM((HALO_ROWS, d_conv), _f32),
            pltpu.VMEM((SUBLANES, tm + HALO_ROWS, d_conv), _f32),
        ],
        compiler_params=_cparams(("parallel", "arbitrary")),
        name="in_proj",
    )(x, g, w_f, w_nat, w_tr, bf_ext, conv_w, conv_b, ln_g, ln_b)


def _attn_kernel(qt_ref, k_ref, aug_ref, vt_ref, o_ref, q_ext_ref, s_ref, mx_ref, p_ref, acc_ref, m_ref,
                 alpha_ref, *, t):
    group = pl.program_id(1)
    qi = pl.program_id(2)
    t0 = qi * t
    hd = HEAD_DIM
    acc_rows = acc_ref.shape[1]
    n_heads = acc_ref.shape[0]
    heads = range(n_heads)

    row = lax.broadcasted_iota(jnp.int32, (LANES, t), 0)
    for hh in heads:
        head = n_heads * group + hh
        lo = (hh % 2) * hd
        qt = qt_ref[(hh // 2) * LANES:(hh // 2 + 1) * LANES, :]
        top = jnp.where((row >= lo) & (row < lo + hd), qt, jnp.zeros_like(qt))
        sel = ((row >= DECAY_SPLITS * head) & (row < DECAY_SPLITS * (head + 1))).astype(_bf16)
        q_ext_ref[hh, 0:LANES, :] = top
        q_ext_ref[hh, LANES:, :] = sel
    acc_ref[...] = jnp.zeros_like(acc_ref)
    m_ref[...] = jnp.full_like(m_ref, NEG_BIG)
    alpha_ref[...] = jnp.ones_like(alpha_ref)
    p_ref[1] = jnp.zeros_like(p_ref[1])
    ones_rows = jnp.ones((acc_rows - hd, t), _bf16)

    def logits_into(slot, s0):
        aug = aug_ref[pl.ds(s0, t), :]
        for hh in heads:
            k_pair = k_ref[pl.ds(s0, t), (hh // 2) * LANES:(hh // 2 + 1) * LANES]
            kk = jnp.concatenate([k_pair, aug], axis=1)
            st = jnp.dot(kk, q_ext_ref[hh], preferred_element_type=_f32)
            s_ref[slot, hh] = st
            mx_ref[slot, hh] = jnp.max(st, axis=0, keepdims=True)

    def softmax_stage(slot, hh, masked):
        def tile():
            st = s_ref[slot, hh]
            if masked:
                valid = (lax.broadcasted_iota(jnp.int32, (t, t), 0)
                         <= lax.broadcasted_iota(jnp.int32, (t, t), 1))
                st = jnp.where(valid, st, NEG_BIG)
            return st
        tile_max = jnp.max(tile(), axis=0, keepdims=True) if masked else mx_ref[slot, hh]
        m_old = m_ref[hh]
        m_new = jnp.maximum(m_old, tile_max)
        m_ref[hh] = m_new
        return jnp.exp2(tile() - m_new).astype(_bf16), jnp.exp2(m_old - m_new)

    def weighted_values(hh, s0, p):
        v_ext = jnp.concatenate([vt_ref[hh * hd:(hh + 1) * hd, pl.ds(s0, t)], ones_rows], axis=0)
        return jnp.dot(v_ext, p, preferred_element_type=_f32)

    def trip(j, slot):
        logits_into(1 - slot, pl.multiple_of((j + 1) * t, t))
        s_prev = pl.multiple_of(jnp.maximum(j - 1, 0) * t, t)
        for hh in heads:
            p, alpha = softmax_stage(slot, hh, masked=False)
            acc_ref[hh] = alpha_ref[hh] * acc_ref[hh] + weighted_values(hh, s_prev, p_ref[1 - slot, hh])
            alpha_ref[hh] = alpha
            p_ref[slot, hh] = p

    def finish(slot):
        s_prev = pl.multiple_of(jnp.maximum(qi - 1, 0) * t, t)
        for hh in heads:
            p, alpha = softmax_stage(slot, hh, masked=True)
            acc = alpha_ref[hh] * acc_ref[hh] + weighted_values(hh, s_prev, p_ref[1 - slot, hh])
            acc = alpha * acc + weighted_values(hh, pl.multiple_of(t0, t), p)
            inv_l = 1.0 / acc[hd:hd + 1, :]
            o_ref[hh * hd:(hh + 1) * hd, :] = (acc[:hd, :] * inv_l).astype(o_ref.dtype)

    logits_into(0, 0)

    def two_trips(i, _):
        trip(2 * i, 0)
        trip(2 * i + 1, 1)
        return 0

    lax.fori_loop(0, lax.shift_right_logical(qi, 1), two_trips, 0)
    odd = (qi & 1) == 1

    @pl.when(odd)
    def _():
        trip(qi - 1, 0)
        finish(1)

    @pl.when(jnp.logical_not(odd))
    def _():
        finish(0)


def _attention(qt, k, aug, vt, *, t, hps):
    b, da, s = qt.shape
    w = hps * HEAD_DIM
    acc_rows = HEAD_DIM + BF16_ROWS
    kern = functools.partial(_attn_kernel, t=t)
    return pl.pallas_call(
        kern,
        grid=(b, da // w, s // t),
        in_specs=[
            pl.BlockSpec((None, w, t), lambda i, g, j: (i, g, j)),
            pl.BlockSpec((None, s, w), lambda i, g, j: (i, 0, g), pipeline_mode=pl.Buffered(1)),
            pl.BlockSpec((None, s, LANES), lambda i, g, j: (i, 0, 0), pipeline_mode=pl.Buffered(1)),
            pl.BlockSpec((None, w, s), lambda i, g, j: (i, g, 0), pipeline_mode=pl.Buffered(1)),
        ],
        out_specs=pl.BlockSpec((None, w, t), lambda i, g, j: (i, g, j)),
        out_shape=jax.ShapeDtypeStruct((b, da, s), _bf16),
        scratch_shapes=[
            pltpu.VMEM((hps, 2 * LANES, t), _bf16),
            pltpu.VMEM((2, hps, t, t), _f32),
            pltpu.VMEM((2, hps, 1, t), _f32),
            pltpu.VMEM((2, hps, t, t), _bf16),
            pltpu.VMEM((hps, acc_rows, t), _f32),
            pltpu.VMEM((hps, 1, t), _f32),
            pltpu.VMEM((hps, 1, t), _f32),
        ],
        compiler_params=_cparams(("parallel", "parallel", "arbitrary")),
        name="attention",
    )(qt, k, aug, vt)


def _conv_tile(u_cur, halo, w_ref, cb_ref, lg_ref, lb_ref, win_ref, out_ref):
    tc = u_cur.shape[0]
    win_ref[0, 0:HALO_ROWS, :] = halo
    win_ref[0, HALO_ROWS:, :] = u_cur.astype(_f32)
    n_shift = tc + HALO_ROWS - SUBLANES
    for r in range(1, SUBLANES):
        win_ref[r, 0:n_shift, :] = win_ref[0, r:r + n_shift, :]
    lead = HALO_ROWS - (CONV_WIDTH - 1)
    groups = (CONV_CHUNK // SUBLANES, SUBLANES, cb_ref.shape[1])
    for r0 in range(0, tc, CONV_CHUNK):
        acc = jnp.broadcast_to(cb_ref[...], groups)
        for k in range(CONV_WIDTH):
            phase, base = (lead + k) % SUBLANES, (lead + k) // SUBLANES * SUBLANES
            tap = win_ref[phase, r0 + base:r0 + base + CONV_CHUNK, :].reshape(groups)
            acc = acc + w_ref[k] * tap
        acc = acc.reshape(CONV_CHUNK, groups[2])
        mu = jnp.mean(acc, axis=-1, keepdims=True)
        cen = acc - mu
        var = jnp.mean(cen * cen, axis=-1, keepdims=True)
        y = cen * lax.rsqrt(var + EPS) * lg_ref[...] + lb_ref[...]
        out_ref[r0:r0 + CONV_CHUNK, :] = (y * jax.nn.sigmoid(y)).astype(out_ref.dtype)


def _mix_ffn_kernel(x_ref, conv_ref, at_ref, woc_ref, woa_ref, gpost_ref, gpre_ref, wg_ref, wu_ref, wd_ref,
                    gout_ref, o_ref):
    tm = x_ref.shape[0]
    halves = [slice(0, tm // 2), slice(tm // 2, tm)]
    mix = [jnp.dot(conv_ref[r, :], woc_ref[...], preferred_element_type=_f32)
           + lax.dot_general(at_ref[:, r], woa_ref[...], (((0,), (0,)), ((), ())),
                             preferred_element_type=_f32) for r in halves]
    x1, gate, up = [], [], []
    for r, mx in zip(halves, mix):
        x1.append(x_ref[r, :] + mx * _rms_scale(mx) * gpost_ref[...])
        h = (x1[-1] * _rms_scale(x1[-1]) * gpre_ref[...]).astype(_bf16)
        gate.append(jnp.dot(h, wg_ref[...], preferred_element_type=_f32))
        up.append(jnp.dot(h, wu_ref[...], preferred_element_type=_f32))
    for r, x1_r, g_r, u_r in zip(halves, x1, gate, up):
        act = (g_r * jax.nn.sigmoid(g_r) * u_r).astype(_bf16)
        y = jnp.dot(act, wd_ref[...], preferred_element_type=_f32)
        o_ref[r, :] = x1_r + y * _rms_scale(y) * gout_ref[...]


def _mix_ffn(x, conv_out, attn_t, wo_c, wo_a, g_post, g_pre, w_gate, w_up, w_down, g_out, *, tm):
    b, s, d = x.shape
    dc = conv_out.shape[2]
    da = attn_t.shape[1]
    dff = w_gate.shape[1]
    return pl.pallas_call(
        _mix_ffn_kernel,
        grid=(b, s // tm),
        in_specs=[
            pl.BlockSpec((None, tm, d), lambda i, j: (i, j, 0)),
            pl.BlockSpec((None, tm, dc), lambda i, j: (i, j, 0)),
            pl.BlockSpec((None, da, tm), lambda i, j: (i, 0, j)),
            _resident((dc, d)),
            _resident((da, d)),
            _resident((1, d)),
            _resident((1, d)),
            _resident((d, dff)),
            _resident((d, dff)),
            _resident((dff, d)),
            _resident((1, d)),
        ],
        out_specs=pl.BlockSpec((None, tm, d), lambda i, j: (i, j, 0)),
        out_shape=jax.ShapeDtypeStruct((b, s, d), x.dtype),
        compiler_params=_cparams(("parallel", "parallel")),
        name="mix_ffn",
    )(x, conv_out, attn_t, wo_c, wo_a, g_post, g_pre, w_gate, w_up, w_down, g_out)


class _Tiles(NamedTuple):
    proj_rows: int
    attn: int
    ffn_rows: int


def _tile_sizes(s):
    tiles = _Tiles(proj_rows=min(1024, s), attn=min(512, s), ffn_rows=min(512, s))
    assert all(s % n == 0 and n % LANES == 0 for n in tiles) and tiles.ffn_rows % CONV_CHUNK == 0
    return tiles


def _layer(x, g_mix_pre, w_in, b_forget, conv_w, conv_b, conv_ln_g, conv_ln_b, w_out, g_mix_post,
           g_ffn_pre, w_gate, w_up, w_down, g_ffn_post):
    b, s, d = x.shape
    d_conv = conv_w.shape[1]
    n_heads = b_forget.shape[0]
    d_attn = n_heads * HEAD_DIM
    n_cols = DECAY_SPLITS * n_heads
    assert 2 * d_conv + 3 * d_attn + n_heads == w_in.shape[1]
    assert n_cols <= LANES and d_attn % LANES == 0

    tiles = _tile_sizes(s)

    o_q, o_k, o_v, o_f = 2 * d_conv, 2 * d_conv + d_attn, 2 * d_conv + 2 * d_attn, 2 * d_conv + 3 * d_attn
    w_f = jnp.repeat(w_in[:, o_f:], DECAY_SPLITS, axis=1)
    w_f = jnp.pad(w_f, ((0, 0), (0, LANES - n_cols))).astype(_bf16)
    w_nat = jnp.concatenate([w_in[:, :o_q], w_in[:, o_k:o_v]], axis=1).astype(_bf16)
    w_tr = jnp.concatenate([w_in[:, o_q:o_k] * (HEAD_DIM ** -0.5 * LOG2_E), w_in[:, o_v:o_f]], axis=1).T.astype(_bf16)
    bf_ext = jnp.pad(jnp.repeat(b_forget.astype(_f32), DECAY_SPLITS), (0, LANES - n_cols))[None, :]

    row = lambda v: v.astype(_f32)[None, :]
    over_sublanes = lambda v: jnp.broadcast_to(v.astype(_f32)[..., None, :], v.shape[:-1] + (SUBLANES, v.shape[-1]))
    conv_out, k, aug, qt, vt = _in_proj(x, row(g_mix_pre), w_f, w_nat, w_tr, bf_ext, over_sublanes(conv_w),
                                        over_sublanes(conv_b), row(conv_ln_g), row(conv_ln_b), d_conv=d_conv,
                                        d_attn=d_attn, n_cols=n_cols, tm=tiles.proj_rows)
    attn_t = _attention(qt, k, aug, vt, t=tiles.attn, hps=n_heads)
    return _mix_ffn(x, conv_out, attn_t, w_out[:d_conv].astype(_bf16), w_out[d_conv:].astype(_bf16), row(g_mix_post),
                    row(g_ffn_pre), w_gate.astype(_bf16), w_up.astype(_bf16), w_down.astype(_bf16),
                    row(g_ffn_post), tm=tiles.ffn_rows)


def kernel(x, g_mix_pre, w_in, b_forget, conv_w, conv_b, conv_ln_g, conv_ln_b, w_out, g_mix_post,
           g_ffn_pre, w_gate, w_up, w_down, g_ffn_post):
    for l in range(g_mix_pre.shape[0]):
        x = _layer(x, g_mix_pre[l], w_in[l], b_forget[l], conv_w[l], conv_b[l], conv_ln_g[l],
                   conv_ln_b[l], w_out[l], g_mix_post[l], g_ffn_pre[l], w_gate[l], w_up[l],
                   w_down[l], g_ffn_post[l])
    return x
```
